```python
import jax
import jax.numpy as jnp
from jax import lax
import numpy as np

D_MODEL = 2048
BATCH = 8
SEQ = 2048
DEPTH = 2

GRID_W = 64
CTX_LEN = 256
N_MOD = 9
D_FF = 5632
D_SC = D_MODEL // 2
SC_WIDTH = 3
D_CC = D_MODEL // 2
CC_WIDTH = 31
D_EVEN_IN = 3 * D_SC + 2 * D_CC
HEAD_DIM = 64
D_NA = D_MODEL // 2
NA_HEADS = D_NA // HEAD_DIM
WIN_ROWS = 8
WIN_COLS = 16
D_LRU = D_MODEL // 2
LRU_BLOCKS = 16
LRU_CONV = 4
LRU_C = 8.0
D_ODD_IN = 3 * D_NA + 2 * D_LRU
D_MIX = D_MODEL
EPS = 1e-6
NEG_INF = -1e30

kernel_name = 'hybrid_dit_shortconv_conformer_natten_rglru'


def rms_norm(x, g):
    xf = x.astype(jnp.float32)
    y = xf * lax.rsqrt(jnp.mean(xf * xf, axis=-1, keepdims=True) + EPS)
    return (y * g.astype(jnp.float32)).astype(x.dtype)


def layer_norm(x, g, b):
    xf = x.astype(jnp.float32)
    mu = jnp.mean(xf, axis=-1, keepdims=True)
    xc = xf - mu
    y = xc * lax.rsqrt(jnp.mean(xc * xc, axis=-1, keepdims=True) + EPS)
    return (y * g.astype(jnp.float32) + b.astype(jnp.float32)).astype(x.dtype)


def modulate(x, g, shift, scale):
    return rms_norm(x, g) * (1.0 + scale) + shift


def swiglu_ffn(h, w_in, w_out):
    gate, up = jnp.split(h @ w_in, 2, axis=-1)
    return (jax.nn.silu(gate) * up) @ w_out


def half_ffn(x, g, mod, w_in, w_out):
    shift, scale, gate = mod
    return x + 0.5 * gate * swiglu_ffn(modulate(x, g, shift, scale), w_in, w_out)


def dwconv(u, w, b):
    k, ch = w.shape
    lo = (k - 1) // 2
    y = lax.conv_general_dilated(u, w[:, None, :].astype(u.dtype), (1,), [(lo, k - 1 - lo)],
                                 dimension_numbers=('NWC', 'WIO', 'NWC'), feature_group_count=ch)
    return y + b


def even_mixer(h, w_in, sc_w, sc_b, cc_w, cc_b, cc_ln_g, cc_ln_b, w_out):
    z = h @ w_in
    b_gate, c_gate, val, glu_v, glu_g = jnp.split(
        z, [D_SC, 2 * D_SC, 3 * D_SC, 3 * D_SC + D_CC], axis=-1)
    y_sc = b_gate * dwconv(c_gate * val, sc_w, sc_b)
    u = dwconv(glu_v * jax.nn.sigmoid(glu_g), cc_w, cc_b)
    y_cc = jax.nn.silu(layer_norm(u, cc_ln_g, cc_ln_b))
    return jnp.concatenate([y_sc, y_cc], axis=-1) @ w_out


def neighbourhood_attention(q, k, v, k_ctx, v_ctx, rpb):
    bsz, seq, nh, hd = q.shape
    rows = seq // GRID_W
    kr = min(WIN_ROWS, rows)
    qg = jnp.moveaxis(q.reshape(bsz, rows, GRID_W, nh, hd), 1, 0)
    kg = k.reshape(bsz, rows, GRID_W, nh, hd)
    vg = v.reshape(bsz, rows, GRID_W, nh, hd)
    col = jnp.arange(GRID_W)
    col_start = jnp.clip(col - WIN_COLS // 2, 0, GRID_W - WIN_COLS)
    col_in = (col[None, :] >= col_start[:, None]) & (col[None, :] < col_start[:, None] + WIN_COLS)
    dcol = jnp.clip(col[None, :] - col[:, None] + WIN_COLS - 1, 0, 2 * WIN_COLS - 2)
    bias_cols = jnp.where(col_in, rpb.astype(jnp.float32)[:, :, dcol], NEG_INF)
    n_loc = kr * GRID_W

    def row_block(args):
        q_row, r = args
        r0 = jnp.clip(r - kr // 2, 0, rows - kr)
        k_blk = lax.dynamic_slice_in_dim(kg, r0, kr, axis=1).reshape(bsz, n_loc, nh, hd)
        v_blk = lax.dynamic_slice_in_dim(vg, r0, kr, axis=1).reshape(bsz, n_loc, nh, hd)
        drow = r0 + jnp.arange(kr) - r + WIN_ROWS - 1
        bias = jnp.moveaxis(bias_cols[:, drow], 1, 2).reshape(nh, GRID_W, n_loc)
        s_loc = jnp.einsum('bqhd,bkhd->bhqk', q_row, k_blk).astype(jnp.float32) + bias
        s_ctx = jnp.einsum('bqhd,bkhd->bhqk', q_row, k_ctx).astype(jnp.float32)
        p = jax.nn.softmax(jnp.concatenate([s_loc, s_ctx], axis=-1), axis=-1).astype(v.dtype)
        return (jnp.einsum('bhqk,bkhd->bqhd', p[..., :n_loc], v_blk)
                + jnp.einsum('bhqk,bkhd->bqhd', p[..., n_loc:], v_ctx))

    out = lax.map(row_block, (qg, jnp.arange(rows)))
    return jnp.moveaxis(out, 0, 1).reshape(bsz, seq, nh * hd)


def context_attention(q, k, v):
    s = jnp.einsum('bqhd,bkhd->bhqk', q, k).astype(jnp.float32)
    p = jax.nn.softmax(s, axis=-1).astype(v.dtype)
    o = jnp.einsum('bhqk,bkhd->bqhd', p, v)
    return o.reshape(o.shape[0], o.shape[1], -1)


def rglru(x, w_gate, b_gate, lam, h0, reverse):
    bsz, t, _ = x.shape
    xb = x.reshape(bsz, t, LRU_BLOCKS, D_LRU // LRU_BLOCKS)
    gates = jnp.einsum('btni,gnij->gbtnj', xb, w_gate).reshape(2, bsz, t, D_LRU)
    gates = (gates + b_gate[:, None, None, :]).astype(jnp.float32)
    r = jax.nn.sigmoid(gates[0])
    i = jax.nn.sigmoid(gates[1])
    log_a = -LRU_C * r * jax.nn.softplus(-lam.astype(jnp.float32))
    a = jnp.exp(log_a)
    u = jnp.sqrt(-jnp.expm1(2.0 * log_a)) * i * x.astype(jnp.float32)
    start = -1 if reverse else 0
    end = 0 if reverse else -1
    u = u.at[:, start].add(a[:, start] * h0)

    def combine(e1, e2):
        a1, b1 = e1
        a2, b2 = e2
        return a1 * a2, a2 * b1 + b2

    _, h = lax.associative_scan(combine, (a, u), reverse=reverse, axis=1)
    return h, h[:, end]


def odd_mixer(h_ctx, h_lat, w_in, q_g, k_g, rpb, conv_w, conv_b, gate_w, gate_b, lam, w_out, ctx_out):
    bsz, seq, _ = h_lat.shape
    heads = lambda t: t.reshape(t.shape[0], t.shape[1], NA_HEADS, HEAD_DIM)
    scale = HEAD_DIM ** -0.5
    q, k, v, xr, gr = jnp.split(h_lat @ w_in, [D_NA, 2 * D_NA, 3 * D_NA, 3 * D_NA + D_LRU], axis=-1)
    if ctx_out:
        qc, kc, vc, xrc, grc = jnp.split(h_ctx @ w_in, [D_NA, 2 * D_NA, 3 * D_NA, 3 * D_NA + D_LRU], axis=-1)
    else:
        kc, vc, xrc = jnp.split(h_ctx @ w_in[:, D_NA:3 * D_NA + D_LRU], [D_NA, 2 * D_NA], axis=-1)
    qh = rms_norm(heads(q), q_g) * scale
    kh = rms_norm(heads(k), k_g)
    kch = rms_norm(heads(kc), k_g)
    vch = heads(vc)
    o_lat = neighbourhood_attention(qh, kh, heads(v), kch, vch, rpb)
    xr = dwconv(xr, conv_w, conv_b)
    xrc = dwconv(xrc, conv_w, conv_b)
    h0 = jnp.zeros((bsz, D_LRU), jnp.float32)
    hc_f, s_f = rglru(xrc, gate_w[0], gate_b[0], lam[0], h0, False)
    hc_b, s_b = rglru(xrc, gate_w[1], gate_b[1], lam[1], h0, True)
    hl_f, _ = rglru(xr, gate_w[0], gate_b[0], lam[0], s_f, False)
    hl_b, _ = rglru(xr, gate_w[1], gate_b[1], lam[1], s_b, True)
    r_lat = (hl_f + hl_b).astype(h_lat.dtype) * jax.nn.gelu(gr)
    y_lat = jnp.concatenate([o_lat, r_lat], axis=-1) @ w_out
    if ctx_out:
        o_ctx = context_attention(rms_norm(heads(qc), q_g) * scale, kch, vch)
        r_ctx = (hc_f + hc_b).astype(h_ctx.dtype) * jax.nn.gelu(grc)
        y_ctx = jnp.concatenate([o_ctx, r_ctx], axis=-1) @ w_out
    else:
        y_ctx = None
    return y_ctx, y_lat


def setup_inputs(seed: int = 0) -> dict:
    key = jax.random.key(seed)
    keys = jax.random.split(key, 32)
    ks = iter([keys[i] for i in range(32)])
    f32 = jnp.float32
    n_even = (DEPTH + 1) // 2
    n_odd = DEPTH // 2

    def nrm(shape, scale):
        return jax.random.normal(next(ks), shape, f32) * scale

    def gain(shape):
        return 1.0 + nrm(shape, 0.02)

    lru_u = jax.random.uniform(next(ks), (n_odd, 2, D_LRU), f32, 0.9, 0.999)
    lru_a = lru_u ** (1.0 / LRU_C)
    lru_lam = jnp.log(lru_a) - jnp.log1p(-lru_a)
    bs = D_LRU // LRU_BLOCKS
    return {
        'x': nrm((BATCH, SEQ, D_MODEL), 1.0),
        'c': nrm((BATCH, D_MODEL), 1.0),
        'ctx': nrm((BATCH, CTX_LEN, D_MODEL), 1.0),
        'c_ctx': nrm((D_MODEL,), 1.0),
        'w_mod': nrm((DEPTH, D_MODEL, N_MOD * D_MODEL), 0.5 * D_MODEL ** -0.5),
        'b_mod': nrm((DEPTH, N_MOD * D_MODEL), 0.02),
        'norm_g': gain((DEPTH, 3, D_MODEL)),
        'ffn_w_in': nrm((DEPTH, 2, D_MODEL, 2 * D_FF), D_MODEL ** -0.5),
        'ffn_w_out': nrm((DEPTH, 2, D_FF, D_MODEL), D_FF ** -0.5),
        'ev_w_in': nrm((n_even, D_MODEL, D_EVEN_IN), D_MODEL ** -0.5),
        'sc_w': nrm((n_even, SC_WIDTH, D_SC), SC_WIDTH ** -0.5),
        'sc_b': nrm((n_even, D_SC), 0.02),
        'cc_w': nrm((n_even, CC_WIDTH, D_CC), CC_WIDTH ** -0.5),
        'cc_b': nrm((n_even, D_CC), 0.02),
        'cc_ln_g': gain((n_even, D_CC)),
        'cc_ln_b': nrm((n_even, D_CC), 0.02),
        'ev_w_out': nrm((n_even, D_MIX, D_MODEL), D_MIX ** -0.5),
        'od_w_in': nrm((n_odd, D_MODEL, D_ODD_IN), D_MODEL ** -0.5),
        'q_norm_g': gain((n_odd, HEAD_DIM)),
        'k_norm_g': gain((n_odd, HEAD_DIM)),
        'na_rpb': nrm((n_odd, NA_HEADS, 2 * WIN_ROWS - 1, 2 * WIN_COLS - 1), 0.1),
        'lru_conv_w': nrm((n_odd, LRU_CONV, D_LRU), LRU_CONV ** -0.5),
        'lru_conv_b': nrm((n_odd, D_LRU), 0.02),
        'lru_gate_w': nrm((n_odd, 2, 2, LRU_BLOCKS, bs, bs), bs ** -0.5),
        'lru_gate_b': nrm((n_odd, 2, 2, D_LRU), 0.02),
        'lru_lam': lru_lam,
        'od_w_out': nrm((n_odd, D_MIX, D_MODEL), D_MIX ** -0.5),
    }


def reference(x, c, ctx, c_ctx, w_mod, b_mod, norm_g, ffn_w_in, ffn_w_out,
              ev_w_in, sc_w, sc_b, cc_w, cc_b, cc_ln_g, cc_ln_b, ev_w_out,
              od_w_in, q_norm_g, k_norm_g, na_rpb, lru_conv_w, lru_conv_b,
              lru_gate_w, lru_gate_b, lru_lam, od_w_out):
    silu_c = jax.nn.silu(c)
    silu_cc = jax.nn.silu(c_ctx)
    x_lat, x_ctx = x, ctx
    for l in range(DEPTH):
        last = l == DEPTH - 1
        odd = l % 2 == 1
        j = l // 2
        ctx_in = odd or not last
        ctx_out = not last
        m_lat = jnp.split((silu_c @ w_mod[l] + b_mod[l])[:, None, :], N_MOD, axis=-1)
        m_ctx = jnp.split(silu_cc @ w_mod[l] + b_mod[l], N_MOD, axis=-1)
        g = norm_g[l]
        x_lat = half_ffn(x_lat, g[0], m_lat[0:3], ffn_w_in[l, 0], ffn_w_out[l, 0])
        if ctx_in:
            x_ctx = half_ffn(x_ctx, g[0], m_ctx[0:3], ffn_w_in[l, 0], ffn_w_out[l, 0])
        h_lat = modulate(x_lat, g[1], m_lat[3], m_lat[4])
        h_ctx = modulate(x_ctx, g[1], m_ctx[3], m_ctx[4]) if ctx_in else None
        if odd:
            y_ctx, y_lat = odd_mixer(h_ctx, h_lat, od_w_in[j], q_norm_g[j], k_norm_g[j], na_rpb[j],
                                     lru_conv_w[j], lru_conv_b[j], lru_gate_w[j], lru_gate_b[j],
                                     lru_lam[j], od_w_out[j], ctx_out)
        else:
            y_lat = even_mixer(h_lat, ev_w_in[j], sc_w[j], sc_b[j], cc_w[j], cc_b[j],
                               cc_ln_g[j], cc_ln_b[j], ev_w_out[j])
            y_ctx = (even_mixer(h_ctx, ev_w_in[j], sc_w[j], sc_b[j], cc_w[j], cc_b[j],
                                cc_ln_g[j], cc_ln_b[j], ev_w_out[j]) if ctx_out else None)
        x_lat = x_lat + m_lat[5] * y_lat
        x_lat = half_ffn(x_lat, g[2], m_lat[6:9], ffn_w_in[l, 1], ffn_w_out[l, 1])
        if ctx_out:
            x_ctx = x_ctx + m_ctx[5] * y_ctx
            x_ctx = half_ffn(x_ctx, g[2], m_ctx[6:9], ffn_w_in[l, 1], ffn_w_out[l, 1])
    return x_lat
```

```python
import functools

import jax
import jax.numpy as jnp
from jax import lax
from jax.experimental import pallas as pl
from jax.experimental.pallas import tpu as pltpu

f32 = jnp.float32
bf16 = jnp.bfloat16

GRID_W = 64
LRU_C = 8.0
EPS = 1e-6
NEG_INF = -1e30
N_MOD = 9
MXU_DIM = 256
VMEM_LIMIT_BYTES = 56 * 1024 * 1024


def _divisor(n, pref):
    d = min(n, pref)
    while n % d:
        d -= 1
    return d


def _params(*sem):
    return pltpu.CompilerParams(dimension_semantics=sem, vmem_limit_bytes=VMEM_LIMIT_BYTES)


def _silu(x):
    return x * jax.nn.sigmoid(x)


def _gelu_tanh(x):
    return 0.5 * x * (1.0 + jnp.tanh(0.7978845608028654 * (x + 0.044715 * (x * x * x))))


def _modulate(x, g, shift, scale):
    ms = jnp.mean(x * x, axis=-1, keepdims=True)
    y = x * lax.rsqrt(ms + EPS) * g
    return y * (1.0 + scale) + shift


def _dot(a, b):
    return jnp.dot(a, b, preferred_element_type=f32)


def _mod_spec(D, row, k):
    return pl.BlockSpec((1, 1, D), lambda i, j: (row(i), 0, k))


def _mod_kernel(c_ref, w_ref, b_ref, o_ref):
    s = _silu(c_ref[...]).astype(bf16)
    o_ref[0] = _dot(s, w_ref[0].astype(bf16)) + b_ref[0]


def _mod_table(cvec, w_mod, b_mod):
    depth, D, N = w_mod.shape
    R = cvec.shape[0]
    tn = _divisor(N, 1024)
    return pl.pallas_call(
        _mod_kernel,
        grid=(depth, N // tn),
        in_specs=[pl.BlockSpec((R, D), lambda l, n: (0, 0)),
                  pl.BlockSpec((1, D, tn), lambda l, n: (l, 0, n)),
                  pl.BlockSpec((1, 1, tn), lambda l, n: (l, 0, n))],
        out_specs=pl.BlockSpec((1, R, tn), lambda l, n: (l, 0, n)),
        out_shape=jax.ShapeDtypeStruct((depth, R, N), f32),
        compiler_params=_params("arbitrary", "arbitrary"),
        name="mod_table",
    )(cvec, w_mod, b_mod.reshape(depth, 1, N))


def _ffn_kernel(x_ref, g_ref, sh_ref, sc_ref, gt_ref, wg_ref, wu_ref, wo_ref, o_ref, hm_ref, *, nf):
    f = pl.program_id(1)

    @pl.when(f == 0)
    def _():
        hm_ref[...] = _modulate(x_ref[...], g_ref[...], sh_ref[0], sc_ref[0]).astype(bf16)

    h = hm_ref[...]
    gate = _dot(h, wg_ref[...])
    up = _dot(h, wu_ref[...])
    part = _dot((_silu(gate) * up).astype(bf16), wo_ref[...])

    @pl.when(f == 0)
    def _():
        o_ref[...] = part

    @pl.when(f > 0)
    def _():
        o_ref[...] += part

    @pl.when(f == nf - 1)
    def _():
        o_ref[...] = x_ref[...] + 0.5 * gt_ref[0] * o_ref[...]


def _ffn(x, g, mods, k0, w_in, w_out, T, row_const=None):
    M, D = x.shape
    F = w_out.shape[0]
    tm = _divisor(T, 512)
    tf = _divisor(F, 512)
    nt, nf = T // tm, F // tf
    row = (lambda i: row_const) if row_const is not None else (lambda i: i // nt)
    return pl.pallas_call(
        functools.partial(_ffn_kernel, nf=nf),
        grid=(M // tm, nf),
        in_specs=[pl.BlockSpec((tm, D), lambda i, f: (i, 0)),
                  pl.BlockSpec((1, D), lambda i, f: (0, 0)),
                  _mod_spec(D, row, k0), _mod_spec(D, row, k0 + 1), _mod_spec(D, row, k0 + 2),
                  pl.BlockSpec((D, tf), lambda i, f: (0, f)),
                  pl.BlockSpec((D, tf), lambda i, f: (0, nf + f)),
                  pl.BlockSpec((tf, D), lambda i, f: (f, 0))],
        out_specs=pl.BlockSpec((tm, D), lambda i, f: (i, 0)),
        out_shape=jax.ShapeDtypeStruct((M, D), f32),
        scratch_shapes=[pltpu.VMEM((tm, D), bf16)],
        compiler_params=_params("arbitrary", "arbitrary"),
        name="ffn",
    )(x, g, mods, mods, mods, w_in, w_in, w_out)


def _modmm_kernel(x_ref, g_ref, sh_ref, sc_ref, w_ref, o_ref):
    h = _modulate(x_ref[...], g_ref[...], sh_ref[0], sc_ref[0]).astype(bf16)
    o_ref[...] = _dot(h, w_ref[...])


def _modmm(x, g, mods, w, T, row_const=None):
    M, D = x.shape
    N = w.shape[1]
    tm = _divisor(T, 512)
    tn = _divisor(N, 1024)
    nt = T // tm
    row = (lambda i: row_const) if row_const is not None else (lambda i: i // nt)
    return pl.pallas_call(
        _modmm_kernel,
        grid=(N // tn, M // tm),
        in_specs=[pl.BlockSpec((tm, D), lambda n, i: (i, 0)),
                  pl.BlockSpec((1, D), lambda n, i: (0, 0)),
                  pl.BlockSpec((1, 1, D), lambda n, i: (row(i), 0, 3)),
                  pl.BlockSpec((1, 1, D), lambda n, i: (row(i), 0, 4)),
                  pl.BlockSpec((D, tn), lambda n, i: (0, n))],
        out_specs=pl.BlockSpec((tm, tn), lambda n, i: (i, n)),
        out_shape=jax.ShapeDtypeStruct((M, N), f32),
        compiler_params=_params("arbitrary", "arbitrary"),
        name="modmm",
    )(x, g, mods, mods, w)


def _evenin_kernel(x_ref, g_ref, sh_ref, sc_ref, w_ref, bg_ref, p_ref, q_ref, *, ts):
    h = _modulate(x_ref[...], g_ref[...], sh_ref[0], sc_ref[0]).astype(bf16)
    z = _dot(h, w_ref[...])
    bg_ref[...] = z[:, 0:ts]
    p_ref[...] = z[:, ts:2 * ts] * z[:, 2 * ts:3 * ts]
    q_ref[...] = z[:, 3 * ts:4 * ts] * jax.nn.sigmoid(z[:, 4 * ts:5 * ts])


def _evenin(x, g, mods, w_r, ts, T, row_const=None):
    M, D = x.shape
    S = w_r.shape[1] // 5
    tm = _divisor(T, 512)
    nt = T // tm
    row = (lambda i: row_const) if row_const is not None else (lambda i: i // nt)
    out = jax.ShapeDtypeStruct((M, S), f32)
    ospec = pl.BlockSpec((tm, ts), lambda n, i: (i, n))
    return pl.pallas_call(
        functools.partial(_evenin_kernel, ts=ts),
        grid=(S // ts, M // tm),
        in_specs=[pl.BlockSpec((tm, D), lambda n, i: (i, 0)),
                  pl.BlockSpec((1, D), lambda n, i: (0, 0)),
                  pl.BlockSpec((1, 1, D), lambda n, i: (row(i), 0, 3)),
                  pl.BlockSpec((1, 1, D), lambda n, i: (row(i), 0, 4)),
                  pl.BlockSpec((D, 5 * ts), lambda n, i: (0, n))],
        out_specs=[ospec, ospec, ospec],
        out_shape=[out, out, out],
        compiler_params=_params("arbitrary", "arbitrary"),
        name="evenin",
    )(x, g, mods, mods, w_r)


def _even_tm_kernel(bg_ref, pp_ref, p_ref, pn_ref, qp_ref, q_ref, qn_ref, scw_ref, scb_ref, ccw_ref, ccb_ref,
                    lng_ref, lnb_ref, y_ref, pw_ref, qw_ref, *, tt, tc, hq):
    B, S = bg_ref.shape[1], bg_ref.shape[2]
    ksc, kcc = scw_ref.shape[0], ccw_ref.shape[0]
    lo_sc, lo_cc = (ksc - 1) // 2, (kcc - 1) // 2
    j = pl.program_id(0)
    first = j == 0
    last = j == pl.num_programs(0) - 1
    pw_ref[0:1] = jnp.where(first, 0.0, pp_ref[...])
    pw_ref[1:tt + 1] = p_ref[...]
    pw_ref[tt + 1:tt + 2] = jnp.where(last, 0.0, pn_ref[...])
    qw_ref[0:hq] = jnp.where(first, 0.0, qp_ref[...])
    qw_ref[hq:hq + tt] = q_ref[...]
    qw_ref[hq + tt:hq + tt + hq] = jnp.where(last, 0.0, qn_ref[...])

    def body(ci, carry):
        t0 = ci * tc
        acc = scw_ref[0][None] * pw_ref[pl.ds(t0 + 1 - lo_sc, tc)]
        for k in range(1, ksc):
            acc = acc + scw_ref[k][None] * pw_ref[pl.ds(t0 + 1 - lo_sc + k, tc)]
        y_ref[pl.ds(t0, tc), :, 0:S] = bg_ref[pl.ds(t0, tc)] * (acc + scb_ref[...][None])
        u = ccw_ref[0][None] * qw_ref[pl.ds(t0 + hq - lo_cc, tc)]
        for k in range(1, kcc):
            u = u + ccw_ref[k][None] * qw_ref[pl.ds(t0 + hq - lo_cc + k, tc)]
        u = (u + ccb_ref[...][None]).reshape(tc * B, S)
        mu = jnp.mean(u, axis=-1, keepdims=True)
        xc = u - mu
        var = jnp.mean(xc * xc, axis=-1, keepdims=True)
        ln = xc * lax.rsqrt(var + EPS) * lng_ref[...] + lnb_ref[...]
        y_ref[pl.ds(t0, tc), :, S:2 * S] = _silu(ln).reshape(tc, B, S)
        return carry

    lax.fori_loop(0, tt // tc, body, 0)


def _even_tm(bg, p, q, scw, scb, ccw, ccb, lng, lnb):
    T, B, S = bg.shape
    hq = 16
    assert (ccw.shape[0] - 1) // 2 <= hq and (scw.shape[0] - 1) // 2 <= 1
    tt = _divisor(T, 64)
    assert tt % hq == 0
    nj, r = T // tt, tt // hq
    cur = pl.BlockSpec((tt, B, S), lambda j: (j, 0, 0))
    full = lambda a: pl.BlockSpec(a.shape, lambda j: (0,) * a.ndim)
    return pl.pallas_call(
        functools.partial(_even_tm_kernel, tt=tt, tc=4, hq=hq),
        grid=(nj,),
        in_specs=[cur,
                  pl.BlockSpec((1, B, S), lambda j: (jnp.maximum(j * tt - 1, 0), 0, 0)),
                  cur,
                  pl.BlockSpec((1, B, S), lambda j: (jnp.minimum((j + 1) * tt, T - 1), 0, 0)),
                  pl.BlockSpec((hq, B, S), lambda j: (jnp.maximum(j * r - 1, 0), 0, 0)),
                  cur,
                  pl.BlockSpec((hq, B, S), lambda j: (jnp.minimum((j + 1) * r, T // hq - 1), 0, 0)),
                  full(scw), full(scb), full(ccw), full(ccb), full(lng), full(lnb)],
        out_specs=pl.BlockSpec((tt, B, 2 * S), lambda j: (j, 0, 0)),
        out_shape=jax.ShapeDtypeStruct((T, B, 2 * S), f32),
        scratch_shapes=[pltpu.VMEM((tt + 2, B, S), f32), pltpu.VMEM((tt + 2 * hq, B, S), f32)],
        compiler_params=_params("arbitrary"),
        name="even_tm",
    )(bg, p, p, p, q, q, q, scw, scb, ccw, ccb, lng, lnb)


def _outproj_even_kernel(y_ref, w_ref, x_ref, gt_ref, o_ref):
    o_ref[...] = x_ref[...] + gt_ref[0] * _dot(y_ref[...], w_ref[...])


def _outproj_even(y, w, x, mods, T, row_const=None):
    M, D = x.shape
    K = y.shape[1]
    tm = _divisor(T, 256)
    nt = T // tm
    row = (lambda i: row_const) if row_const is not None else (lambda i: i // nt)
    return pl.pallas_call(
        _outproj_even_kernel,
        grid=(M // tm,),
        in_specs=[pl.BlockSpec((tm, K), lambda i: (i, 0)),
                  pl.BlockSpec((K, D), lambda i: (0, 0)),
                  pl.BlockSpec((tm, D), lambda i: (i, 0)),
                  pl.BlockSpec((1, 1, D), lambda i: (row(i), 0, 5))],
        out_specs=pl.BlockSpec((tm, D), lambda i: (i, 0)),
        out_shape=jax.ShapeDtypeStruct((M, D), f32),
        compiler_params=_params("arbitrary"),
        name="outproj_even",
    )(y, w, x, mods)


def _outproj_odd_kernel(o_ref, hf_ref, hb_ref, gr_ref, wa_ref, wb_ref, x_ref, gt_ref, out_ref):
    r = ((hf_ref[...] + hb_ref[...]) * _gelu_tanh(gr_ref[...])).astype(bf16)
    y = _dot(o_ref[...], wa_ref[...]) + _dot(r, wb_ref[...])
    out_ref[...] = x_ref[...] + gt_ref[0] * y


def _outproj_odd(o, hf, hb, z, gr_block, w, x, mods, T):
    M, D = x.shape
    Ka, Kb = o.shape[1], hf.shape[1]
    tm = _divisor(T, 256)
    nt = T // tm
    return pl.pallas_call(
        _outproj_odd_kernel,
        grid=(M // tm,),
        in_specs=[pl.BlockSpec((tm, Ka), lambda i: (i, 0)),
                  pl.BlockSpec((tm, Kb), lambda i: (i, 0)),
                  pl.BlockSpec((tm, Kb), lambda i: (i, 0)),
                  pl.BlockSpec((tm, Kb), lambda i: (i, gr_block)),
                  pl.BlockSpec((Ka, D), lambda i: (0, 0)),
                  pl.BlockSpec((Kb, D), lambda i: (Ka // Kb, 0)),
                  pl.BlockSpec((tm, D), lambda i: (i, 0)),
                  pl.BlockSpec((1, 1, D), lambda i: (i // nt, 0, 5))],
        out_specs=pl.BlockSpec((tm, D), lambda i: (i, 0)),
        out_shape=jax.ShapeDtypeStruct((M, D), f32),
        compiler_params=_params("arbitrary"),
        name="outproj_odd",
    )(o, hf, hb, z, w, w, x, mods)


def _attn_kernel(q_ref, k_ref, v_ref, kc_ref, vc_ref, qg_ref, kg_ref, bias_ref, o_ref,
                 qn_ref, kn_ref, vn_ref, kcn_ref, vcn_ref, *, rows, kr, hd, scale):
    lanes = 2 * hd
    W = GRID_W
    ri = jnp.where(lax.broadcasted_iota(jnp.int32, (lanes, lanes), 0) < hd, 1.0, -1.0)
    ci = jnp.where(lax.broadcasted_iota(jnp.int32, (lanes, lanes), 1) < hd, 1.0, -1.0)
    seg_mean = jnp.where(ri * ci > 0.0, 1.0 / hd, 0.0).astype(bf16)

    def head_norm(x, g):
        sq = x * x
        hi = sq.astype(bf16)
        lo = (sq - hi.astype(f32)).astype(bf16)
        ms = _dot(hi, seg_mean) + _dot(lo, seg_mean)
        return x * lax.rsqrt(ms + EPS) * g

    head0 = lax.broadcasted_iota(jnp.int32, (1, lanes), 1) < hd
    qn = head_norm(q_ref[0], qg_ref[...]) * scale
    qn_ref[0] = jnp.where(head0, qn, 0.0).astype(bf16)
    qn_ref[1] = jnp.where(head0, 0.0, qn).astype(bf16)
    kn_ref[...] = head_norm(k_ref[0], kg_ref[...]).astype(bf16)
    kcn_ref[...] = head_norm(kc_ref[0], kg_ref[...]).astype(bf16)
    vn_ref[...] = v_ref[0].astype(bf16)
    vcn_ref[...] = vc_ref[0].astype(bf16)
    nt_dims = (((1,), (1,)), ((), ()))

    def body(r, carry):
        r0 = jnp.clip(r - kr // 2, 0, rows - kr)
        oi = r - r0
        qs = pl.multiple_of(r * W, W)
        ks = pl.multiple_of(r0 * W, W)
        kw = kn_ref[pl.ds(ks, kr * W), :]
        vw = vn_ref[pl.ds(ks, kr * W), :]
        outs = []
        for hh in range(2):
            qm = qn_ref[hh, pl.ds(qs, W), :]
            sl = lax.dot_general(qm, kw, nt_dims, preferred_element_type=f32) + bias_ref[0, hh, oi]
            sc = lax.dot_general(qm, kcn_ref[...], nt_dims, preferred_element_type=f32)
            m = jnp.maximum(jnp.max(sl, axis=-1, keepdims=True), jnp.max(sc, axis=-1, keepdims=True))
            el = jnp.exp(sl - m)
            ec = jnp.exp(sc - m)
            den = jnp.sum(el, axis=-1, keepdims=True) + jnp.sum(ec, axis=-1, keepdims=True)
            num = _dot(el.astype(bf16), vw) + _dot(ec.astype(bf16), vcn_ref[...])
            outs.append(num / den)
        o_ref[0, pl.ds(qs, W), :] = jnp.where(head0, outs[0], outs[1]).astype(o_ref.dtype)
        return carry

    lax.fori_loop(0, rows, body, 0)


def _bias_table(rpb, rows):
    H = rpb.shape[0]
    R = (rpb.shape[1] + 1) // 2
    Wc = (rpb.shape[2] + 1) // 2
    kr = min(R, rows)
    col = jnp.arange(GRID_W)
    col_start = jnp.clip(col - Wc // 2, 0, GRID_W - Wc)
    col_in = (col[None, :] >= col_start[:, None]) & (col[None, :] < col_start[:, None] + Wc)
    dcol = jnp.clip(col[None, :] - col[:, None] + Wc - 1, 0, 2 * Wc - 2)
    bias_cols = jnp.where(col_in, rpb.astype(f32)[:, :, dcol], NEG_INF)
    drow = jnp.arange(kr)[None, :] - jnp.arange(kr)[:, None] + R - 1
    tab = bias_cols[:, drow]
    return jnp.moveaxis(tab, 2, 3).reshape(H, kr, GRID_W, kr * GRID_W), kr


def _attention(z, zc, qg, kg, rpb, B, L, Cn, n_heads, hd):
    rows = L // GRID_W
    bias, kr = _bias_table(rpb, rows)
    npair = n_heads // 2
    lanes = 2 * hd
    bias = bias.reshape(npair, 2, kr, GRID_W, kr * GRID_W)
    qg2 = jnp.tile(qg, 2).reshape(1, lanes)
    kg2 = jnp.tile(kg, 2).reshape(1, lanes)
    seg = lambda s: pl.BlockSpec((1, L, lanes), lambda p, b: (b, 0, s * npair + p))
    segc = lambda s: pl.BlockSpec((1, Cn, lanes), lambda p, b: (b, 0, s * npair + p))
    return pl.pallas_call(
        functools.partial(_attn_kernel, rows=rows, kr=kr, hd=hd, scale=hd ** -0.5),
        grid=(npair, B),
        in_specs=[seg(0), seg(1), seg(2), segc(0), segc(1),
                  pl.BlockSpec((1, lanes), lambda p, b: (0, 0)),
                  pl.BlockSpec((1, lanes), lambda p, b: (0, 0)),
                  pl.BlockSpec((1, 2, kr, GRID_W, kr * GRID_W), lambda p, b: (p, 0, 0, 0, 0))],
        out_specs=pl.BlockSpec((1, L, lanes), lambda p, b: (b, 0, p)),
        out_shape=jax.ShapeDtypeStruct((B, L, npair * lanes), bf16),
        scratch_shapes=[pltpu.VMEM((2, L, lanes), bf16), pltpu.VMEM((L, lanes), bf16),
                        pltpu.VMEM((L, lanes), bf16), pltpu.VMEM((Cn, lanes), bf16),
                        pltpu.VMEM((Cn, lanes), bf16)],
        compiler_params=_params("arbitrary", "arbitrary"),
        name="attention",
    )(z, z, z, zc, zc, qg2, kg2, bias)


def _lru_kernel(xfp_ref, xf_ref, xfn_ref, xbp_ref, xb_ref, xbn_ref, h0f_ref, h0b_ref, cw_ref, cb_ref,
                wbd_ref, gb_ref, lam_ref, hf_ref, hb_ref, sf_ref, sb_ref,
                xw_ref, xc_ref, a_ref, u_ref, st_ref, *, tt, tc, nj, chunk):
    B, C1 = xf_ref.shape[1], xf_ref.shape[2]
    kw = cw_ref.shape[0]
    lo = (kw - 1) // 2
    hi = kw - 1 - lo
    j = pl.program_id(0)

    @pl.when(j == 0)
    def _():
        st_ref[0] = h0f_ref[...]
        st_ref[1] = h0b_ref[...]

    for d, (xp_ref, xm_ref, xn_ref, jj) in enumerate(((xfp_ref, xf_ref, xfn_ref, j),
                                                     (xbp_ref, xb_ref, xbn_ref, nj - 1 - j))):
        xw_ref[d, 0:lo] = jnp.where(jj == 0, 0.0, xp_ref[...])
        xw_ref[d, lo:lo + tt] = xm_ref[...]
        xw_ref[d, lo + tt:lo + tt + hi] = jnp.where(jj == nj - 1, 0.0, xn_ref[...])

        def conv_body(ci, carry, d=d):
            t0 = ci * tc
            acc = cw_ref[0][None] * xw_ref[d, pl.ds(t0, tc)]
            for k in range(1, kw):
                acc = acc + cw_ref[k][None] * xw_ref[d, pl.ds(t0 + k, tc)]
            xc_ref[d, pl.ds(t0, tc)] = acc + cb_ref[...][None]
            return carry

        lax.fori_loop(0, tt // tc, conv_body, 0)

        lam = lam_ref[d]
        neg = -lam
        softplus = jnp.maximum(neg, 0.0) + jnp.log(1.0 + jnp.exp(-jnp.abs(neg)))
        rate = -LRU_C * softplus
        for c in range(C1 // chunk):
            cs = slice(c * chunk, (c + 1) * chunk)
            xs = xc_ref[d, :, :, cs].reshape(tt * B, chunk)
            xh = xs.astype(bf16)
            r = jax.nn.sigmoid(_dot(xh, wbd_ref[d, 0, c]) + gb_ref[d, 0][:, cs])
            i = jax.nn.sigmoid(_dot(xh, wbd_ref[d, 1, c]) + gb_ref[d, 1][:, cs])
            log_a = rate[:, cs] * r
            a = jnp.exp(log_a)
            mult = jnp.sqrt(jnp.tanh(-log_a) * (a * a + 1.0))
            a_ref[d, :, :, cs] = a.reshape(tt, B, chunk)
            u_ref[d, :, :, cs] = (mult * i * xs).reshape(tt, B, chunk)

    def scan_body(s, carry):
        hf, hb = carry
        tb = tt - 1 - s
        hf = a_ref[0, s] * hf + u_ref[0, s]
        hb = a_ref[1, tb] * hb + u_ref[1, tb]
        hf_ref[s] = hf
        hb_ref[tb] = hb
        return hf, hb

    hf, hb = lax.fori_loop(0, tt, scan_body, (st_ref[0], st_ref[1]))
    st_ref[0] = hf
    st_ref[1] = hb
    sf_ref[...] = hf
    sb_ref[...] = hb


def _lru(x, h0f, h0b, cw, cb, wbd, gb, lam):
    T, B, C1 = x.shape
    kw = cw.shape[0]
    lo = (kw - 1) // 2
    hi = kw - 1 - lo
    assert lo == 1 and hi == 2
    chunk = wbd.shape[-1]
    tt = _divisor(T, 64)
    assert tt % hi == 0
    nj = T // tt
    full = lambda a: pl.BlockSpec(a.shape, lambda j: (0,) * a.ndim)
    state = jax.ShapeDtypeStruct((B, C1), f32)
    seq = jax.ShapeDtypeStruct((T, B, C1), f32)

    def windows(chunk_of):
        return [pl.BlockSpec((lo, B, C1), lambda j: (jnp.maximum(chunk_of(j) * tt - 1, 0), 0, 0)),
                pl.BlockSpec((tt, B, C1), lambda j: (chunk_of(j), 0, 0)),
                pl.BlockSpec((hi, B, C1),
                             lambda j: (jnp.minimum((chunk_of(j) + 1) * (tt // hi), T // hi - 1), 0, 0))]

    fwd = lambda j: j
    bwd = lambda j: nj - 1 - j
    return pl.pallas_call(
        functools.partial(_lru_kernel, tt=tt, tc=8, nj=nj, chunk=chunk),
        grid=(nj,),
        in_specs=windows(fwd) + windows(bwd) + [full(h0f), full(h0b), full(cw), full(cb), full(wbd),
                                                full(gb), full(lam)],
        out_specs=[pl.BlockSpec((tt, B, C1), lambda j: (j, 0, 0)),
                   pl.BlockSpec((tt, B, C1), lambda j: (nj - 1 - j, 0, 0)),
                   pl.BlockSpec((B, C1), lambda j: (0, 0)),
                   pl.BlockSpec((B, C1), lambda j: (0, 0))],
        out_shape=[seq, seq, state, state],
        scratch_shapes=[pltpu.VMEM((2, tt + kw - 1, B, C1), f32), pltpu.VMEM((2, tt, B, C1), f32),
                        pltpu.VMEM((2, tt, B, C1), f32), pltpu.VMEM((2, tt, B, C1), f32),
                        pltpu.VMEM((2, B, C1), f32)],
        compiler_params=_params("arbitrary"),
        name="rglru",
    )(x, x, x, x, x, x, h0f, h0b, cw, cb, wbd, gb, lam)


def _block_diag_chunks(w, chunk):
    *lead, nb, bs, _ = w.shape
    per = chunk // bs
    w6 = w.reshape(*lead, nb // per, per, bs, bs)
    eye = jnp.eye(per, dtype=w.dtype)
    dense = jnp.einsum('...ckij,kl->...ckilj', w6, eye)
    return dense.reshape(*lead, nb // per, chunk, chunk)


def _to_tm(a, B, T):
    return jnp.swapaxes(a.reshape(B, T, a.shape[-1]), 0, 1)


def _to_bm(a):
    T, B, C = a.shape
    return jnp.swapaxes(a, 0, 1).reshape(B * T, C)


def _even_mixer(xs, g, mods, Ts, rows_const, w_in, sc_w, sc_b, cc_w, cc_b, ln_g, ln_b, w_out, B):
    D = w_in.shape[0]
    S = sc_w.shape[1]
    ts = _divisor(S, MXU_DIM)
    w_r = w_in.reshape(D, 5, S // ts, ts).transpose(0, 2, 1, 3).reshape(D, 5 * S).astype(bf16)
    w_o = w_out.astype(bf16)
    bcast = lambda a: jnp.broadcast_to(a[..., None, :], a.shape[:-1] + (B, a.shape[-1]))
    scw, scb, ccw, ccb = bcast(sc_w), bcast(sc_b), bcast(cc_w), bcast(cc_b)
    outs = []
    for x, T, rc in zip(xs, Ts, rows_const):
        bg, p, q = _evenin(x, g, mods, w_r, ts, T, rc)
        y = _even_tm(_to_tm(bg, B, T), _to_tm(p, B, T), _to_tm(q, B, T), scw, scb, ccw, ccb,
                     ln_g.reshape(1, S), ln_b.reshape(1, S))
        outs.append(_outproj_even(_to_bm(y).astype(bf16), w_o, x, mods, T, rc))
    return outs


def _odd_mixer_lat(xl, xc, g, mods, B, L, Cn, w_in, q_g, k_g, rpb, conv_w, conv_b, gate_w, gate_b, lam, w_out):
    D = w_in.shape[0]
    hd = q_g.shape[0]
    n_heads = rpb.shape[0]
    Dn = n_heads * hd
    C1 = conv_w.shape[1]
    w = w_in.astype(bf16)
    z = _modmm(xl, g, mods, w, L)
    zc = _modmm(xc, g, mods, w[:, Dn:3 * Dn + C1], Cn, row_const=B)
    o = _attention(z.reshape(B, L, -1), zc.reshape(B, Cn, -1), q_g, k_g, rpb, B, L, Cn, n_heads, hd)
    chunk = min(MXU_DIM, C1)
    wbd = _block_diag_chunks(gate_w, chunk).astype(bf16)
    bcast = lambda a: jnp.broadcast_to(a[..., None, :], a.shape[:-1] + (B, a.shape[-1]))
    cw, cb = bcast(conv_w), bcast(conv_b)
    gb = gate_b[:, :, None, :]
    lam3 = lam[:, None, :]
    zero = jnp.zeros((B, C1), f32)
    xrc = _to_tm(zc[:, 2 * Dn:2 * Dn + C1], B, Cn)
    _, _, sf, sb = _lru(xrc, zero, zero, cw, cb, wbd, gb, lam3)
    xr = _to_tm(z[:, 3 * Dn:3 * Dn + C1], B, L)
    hf, hb, _, _ = _lru(xr, sf, sb, cw, cb, wbd, gb, lam3)
    assert Dn == C1
    return _outproj_odd(o.reshape(B * L, Dn), _to_bm(hf), _to_bm(hb), z, (3 * Dn + C1) // C1,
                        w_out.astype(bf16), xl, mods, L)


def kernel(x, c, ctx, c_ctx, w_mod, b_mod, norm_g, ffn_w_in, ffn_w_out, ev_w_in, sc_w, sc_b, cc_w, cc_b, cc_ln_g, cc_ln_b, ev_w_out, od_w_in, q_norm_g, k_norm_g, na_rpb, lru_conv_w, lru_conv_b, lru_gate_w, lru_gate_b, lru_lam, od_w_out):
    B, L, D = x.shape
    Cn = ctx.shape[1]
    depth = w_mod.shape[0]
    R = -(-(B + 1) // 8) * 8
    cvec = jnp.zeros((R, D), f32).at[:B].set(c).at[B].set(c_ctx)
    mods_all = _mod_table(cvec, w_mod, b_mod)
    xl = x.reshape(B * L, D)
    xc = ctx.reshape(B * Cn, D)
    for l in range(depth):
        last = l == depth - 1
        odd = l % 2 == 1
        j = l // 2
        ctx_in = odd or not last
        ctx_out = not last
        mods = mods_all[l].reshape(R, 1, N_MOD * D)
        g = norm_g[l].reshape(3, 1, D)
        wi = ffn_w_in[l].astype(bf16)
        wo = ffn_w_out[l].astype(bf16)
        xl = _ffn(xl, g[0], mods, 0, wi[0], wo[0], L)
        if ctx_in:
            xc = _ffn(xc, g[0], mods, 0, wi[0], wo[0], Cn, row_const=B)
        if odd:
            assert ctx_in and not ctx_out
            xl = _odd_mixer_lat(xl, xc, g[1], mods, B, L, Cn, od_w_in[j], q_norm_g[j], k_norm_g[j], na_rpb[j],
                                lru_conv_w[j], lru_conv_b[j], lru_gate_w[j], lru_gate_b[j], lru_lam[j],
                                od_w_out[j])
        else:
            streams = [(xl, L, None)] + ([(xc, Cn, B)] if ctx_out else [])
            outs = _even_mixer([s[0] for s in streams], g[1], mods, [s[1] for s in streams],
                               [s[2] for s in streams], ev_w_in[j], sc_w[j], sc_b[j], cc_w[j], cc_b[j],
                               cc_ln_g[j], cc_ln_b[j], ev_w_out[j], B)
            xl = outs[0]
            if ctx_out:
                xc = outs[1]
        xl = _ffn(xl, g[2], mods, 6, wi[1], wo[1], L)
        if ctx_out:
            xc = _ffn(xc, g[2], mods, 6, wi[1], wo[1], Cn, row_const=B)
    return xl.reshape(B, L, D)
```

```python
import functools

import jax
import jax.numpy as jnp
from jax import lax
from jax.experimental import pallas as pl
from jax.experimental.pallas import tpu as pltpu

f32 = jnp.float32
bf16 = jnp.bfloat16

GRID_W = 64
LRU_C = 8.0
EPS = 1e-6
NEG_INF = -1e30
N_MOD = 9
MXU_DIM = 256
VMEM_LIMIT_BYTES = 56 * 1024 * 1024


def _divisor(n, pref):
    d = min(n, pref)
    while n % d:
        d -= 1
    return d


def _params(*sem):
    return pltpu.CompilerParams(dimension_semantics=sem, vmem_limit_bytes=VMEM_LIMIT_BYTES)


def _silu(x):
    return x * jax.nn.sigmoid(x)


def _gelu_tanh(x):
    return 0.5 * x * (1.0 + jnp.tanh(0.7978845608028654 * (x + 0.044715 * (x * x * x))))


def _modulate(x, g, shift, scale):
    ms = jnp.mean(x * x, axis=-1, keepdims=True)
    y = x * lax.rsqrt(ms + EPS) * g
    return y * (1.0 + scale) + shift


def _dot(a, b):
    return jnp.dot(a, b, preferred_element_type=f32)


def _mod_spec(D, row, k):
    return pl.BlockSpec((1, 1, D), lambda i, j: (row(i), 0, k))


def _mod_kernel(c_ref, w_ref, b_ref, o_ref):
    s = _silu(c_ref[...]).astype(bf16)
    o_ref[0] = _dot(s, w_ref[0].astype(bf16)) + b_ref[0]


def _mod_table(cvec, w_mod, b_mod):
    depth, D, N = w_mod.shape
    R = cvec.shape[0]
    tn = _divisor(N, 1024)
    return pl.pallas_call(
        _mod_kernel,
        grid=(depth, N // tn),
        in_specs=[pl.BlockSpec((R, D), lambda l, n: (0, 0)),
                  pl.BlockSpec((1, D, tn), lambda l, n: (l, 0, n)),
                  pl.BlockSpec((1, 1, tn), lambda l, n: (l, 0, n))],
        out_specs=pl.BlockSpec((1, R, tn), lambda l, n: (l, 0, n)),
        out_shape=jax.ShapeDtypeStruct((depth, R, N), f32),
        compiler_params=_params("arbitrary", "arbitrary"),
        name="mod_table",
    )(cvec, w_mod, b_mod.reshape(depth, 1, N))


def _ffn_kernel(x_ref, g_ref, sh_ref, sc_ref, gt_ref, wg_ref, wu_ref, wo_ref, o_ref, hm_ref, *, nf):
    f = pl.program_id(1)

    @pl.when(f == 0)
    def _():
        hm_ref[...] = _modulate(x_ref[...], g_ref[...], sh_ref[0], sc_ref[0]).astype(bf16)
        o_ref[...] = jnp.zeros_like(o_ref)

    h = hm_ref[...]
    gate = _dot(h, wg_ref[...])
    up = _dot(h, wu_ref[...])
    o_ref[...] += _dot((_silu(gate) * up).astype(bf16), wo_ref[...])

    @pl.when(f == nf - 1)
    def _():
        o_ref[...] = x_ref[...] + 0.5 * gt_ref[0] * o_ref[...]


def _ffn(x, g, mods, k0, w_in, w_out, T, row_const=None):
    M, D = x.shape
    F = w_out.shape[0]
    tm = _divisor(T, 512)
    tf = _divisor(F, 512)
    nt, nf = T // tm, F // tf
    row = (lambda i: row_const) if row_const is not None else (lambda i: i // nt)
    return pl.pallas_call(
        functools.partial(_ffn_kernel, nf=nf),
        grid=(M // tm, nf),
        in_specs=[pl.BlockSpec((tm, D), lambda i, f: (i, 0)),
                  pl.BlockSpec((1, D), lambda i, f: (0, 0)),
                  _mod_spec(D, row, k0), _mod_spec(D, row, k0 + 1), _mod_spec(D, row, k0 + 2),
                  pl.BlockSpec((D, tf), lambda i, f: (0, f)),
                  pl.BlockSpec((D, tf), lambda i, f: (0, nf + f)),
                  pl.BlockSpec((tf, D), lambda i, f: (f, 0))],
        out_specs=pl.BlockSpec((tm, D), lambda i, f: (i, 0)),
        out_shape=jax.ShapeDtypeStruct((M, D), f32),
        scratch_shapes=[pltpu.VMEM((tm, D), bf16)],
        compiler_params=_params("arbitrary", "arbitrary"),
        name="ffn",
    )(x, g, mods, mods, mods, w_in, w_in, w_out)


def _modulate_once(x_ref, g_ref, sh_ref, sc_ref, hm_ref):
    @pl.when(pl.program_id(1) == 0)
    def _():
        hm_ref[...] = _modulate(x_ref[...], g_ref[...], sh_ref[0], sc_ref[0]).astype(bf16)


def _modmm_kernel(x_ref, g_ref, sh_ref, sc_ref, w_ref, o_ref, hm_ref):
    _modulate_once(x_ref, g_ref, sh_ref, sc_ref, hm_ref)
    o_ref[...] = _dot(hm_ref[...], w_ref[...])


def _inproj_specs(D, tm, wn, row):
    return [pl.BlockSpec((tm, D), lambda i, n: (i, 0)),
            pl.BlockSpec((1, D), lambda i, n: (0, 0)),
            _mod_spec(D, row, 3), _mod_spec(D, row, 4),
            pl.BlockSpec((D, wn), lambda i, n: (0, n))]


def _modmm(x, g, mods, w, T, row_const=None):
    M, D = x.shape
    N = w.shape[1]
    tm = _divisor(T, 512)
    tn = _divisor(N, 1024)
    nt = T // tm
    row = (lambda i: row_const) if row_const is not None else (lambda i: i // nt)
    return pl.pallas_call(
        _modmm_kernel,
        grid=(M // tm, N // tn),
        in_specs=_inproj_specs(D, tm, tn, row),
        out_specs=pl.BlockSpec((tm, tn), lambda i, n: (i, n)),
        out_shape=jax.ShapeDtypeStruct((M, N), f32),
        scratch_shapes=[pltpu.VMEM((tm, D), bf16)],
        compiler_params=_params("arbitrary", "arbitrary"),
        name="modmm",
    )(x, g, mods, mods, w)


def _evenin_kernel(x_ref, g_ref, sh_ref, sc_ref, w_ref, bg_ref, p_ref, q_ref, hm_ref, *, ts):
    _modulate_once(x_ref, g_ref, sh_ref, sc_ref, hm_ref)
    z = _dot(hm_ref[...], w_ref[...])
    bg_ref[...] = z[:, 0:ts]
    p_ref[...] = z[:, ts:2 * ts] * z[:, 2 * ts:3 * ts]
    q_ref[...] = z[:, 3 * ts:4 * ts] * jax.nn.sigmoid(z[:, 4 * ts:5 * ts])


def _evenin(x, g, mods, w_r, ts, T, row_const=None):
    M, D = x.shape
    S = w_r.shape[1] // 5
    tm = _divisor(T, 512)
    nt = T // tm
    row = (lambda i: row_const) if row_const is not None else (lambda i: i // nt)
    out = jax.ShapeDtypeStruct((M, S), f32)
    ospec = pl.BlockSpec((tm, ts), lambda i, n: (i, n))
    return pl.pallas_call(
        functools.partial(_evenin_kernel, ts=ts),
        grid=(M // tm, S // ts),
        in_specs=_inproj_specs(D, tm, 5 * ts, row),
        out_specs=[ospec, ospec, ospec],
        out_shape=[out, out, out],
        scratch_shapes=[pltpu.VMEM((tm, D), bf16)],
        compiler_params=_params("arbitrary", "arbitrary"),
        name="evenin",
    )(x, g, mods, mods, w_r)


def _even_tm_kernel(bg_ref, pp_ref, p_ref, pn_ref, qp_ref, q_ref, qn_ref, scw_ref, scb_ref, ccw_ref, ccb_ref,
                    lng_ref, lnb_ref, y_ref, pw_ref, qw_ref, *, tt, tc, hq):
    B, S = bg_ref.shape[1], bg_ref.shape[2]
    ksc, kcc = scw_ref.shape[0], ccw_ref.shape[0]
    lo_sc, lo_cc = (ksc - 1) // 2, (kcc - 1) // 2
    j = pl.program_id(0)
    first = j == 0
    last = j == pl.num_programs(0) - 1
    pw_ref[0:1] = jnp.where(first, 0.0, pp_ref[...])
    pw_ref[1:tt + 1] = p_ref[...]
    pw_ref[tt + 1:tt + 2] = jnp.where(last, 0.0, pn_ref[...])
    qw_ref[0:hq] = jnp.where(first, 0.0, qp_ref[...])
    qw_ref[hq:hq + tt] = q_ref[...]
    qw_ref[hq + tt:hq + tt + hq] = jnp.where(last, 0.0, qn_ref[...])

    def body(ci, carry):
        t0 = ci * tc
        acc = scw_ref[0][None] * pw_ref[pl.ds(t0 + 1 - lo_sc, tc)]
        for k in range(1, ksc):
            acc = acc + scw_ref[k][None] * pw_ref[pl.ds(t0 + 1 - lo_sc + k, tc)]
        y_ref[pl.ds(t0, tc), :, 0:S] = bg_ref[pl.ds(t0, tc)] * (acc + scb_ref[...][None])
        u = ccw_ref[0][None] * qw_ref[pl.ds(t0 + hq - lo_cc, tc)]
        for k in range(1, kcc):
            u = u + ccw_ref[k][None] * qw_ref[pl.ds(t0 + hq - lo_cc + k, tc)]
        u = (u + ccb_ref[...][None]).reshape(tc * B, S)
        mu = jnp.mean(u, axis=-1, keepdims=True)
        xc = u - mu
        var = jnp.mean(xc * xc, axis=-1, keepdims=True)
        ln = xc * lax.rsqrt(var + EPS) * lng_ref[...] + lnb_ref[...]
        y_ref[pl.ds(t0, tc), :, S:2 * S] = _silu(ln).reshape(tc, B, S)
        return carry

    lax.fori_loop(0, tt // tc, body, 0)


def _even_tm(bg, p, q, scw, scb, ccw, ccb, lng, lnb):
    T, B, S = bg.shape
    hq = 16
    assert (ccw.shape[0] - 1) // 2 <= hq and (scw.shape[0] - 1) // 2 <= 1
    tt = _divisor(T, 64)
    assert tt % hq == 0
    nj, r = T // tt, tt // hq
    cur = pl.BlockSpec((tt, B, S), lambda j: (j, 0, 0))
    full = lambda a: pl.BlockSpec(a.shape, lambda j: (0,) * a.ndim)
    return pl.pallas_call(
        functools.partial(_even_tm_kernel, tt=tt, tc=4, hq=hq),
        grid=(nj,),
        in_specs=[cur,
                  pl.BlockSpec((1, B, S), lambda j: (jnp.maximum(j * tt - 1, 0), 0, 0)),
                  cur,
                  pl.BlockSpec((1, B, S), lambda j: (jnp.minimum((j + 1) * tt, T - 1), 0, 0)),
                  pl.BlockSpec((hq, B, S), lambda j: (jnp.maximum(j * r - 1, 0), 0, 0)),
                  cur,
                  pl.BlockSpec((hq, B, S), lambda j: (jnp.minimum((j + 1) * r, T // hq - 1), 0, 0)),
                  full(scw), full(scb), full(ccw), full(ccb), full(lng), full(lnb)],
        out_specs=pl.BlockSpec((tt, B, 2 * S), lambda j: (j, 0, 0)),
        out_shape=jax.ShapeDtypeStruct((T, B, 2 * S), f32),
        scratch_shapes=[pltpu.VMEM((tt + 2, B, S), f32), pltpu.VMEM((tt + 2 * hq, B, S), f32)],
        compiler_params=_params("arbitrary"),
        name="even_tm",
    )(bg, p, p, p, q, q, q, scw, scb, ccw, ccb, lng, lnb)


def _outproj_even_kernel(y_ref, w_ref, x_ref, gt_ref, o_ref):
    o_ref[...] = x_ref[...] + gt_ref[0] * _dot(y_ref[...], w_ref[...])


def _outproj_even(y, w, x, mods, T, row_const=None):
    M, D = x.shape
    K = y.shape[1]
    tm = _divisor(T, 256)
    nt = T // tm
    row = (lambda i: row_const) if row_const is not None else (lambda i: i // nt)
    return pl.pallas_call(
        _outproj_even_kernel,
        grid=(M // tm,),
        in_specs=[pl.BlockSpec((tm, K), lambda i: (i, 0)),
                  pl.BlockSpec((K, D), lambda i: (0, 0)),
                  pl.BlockSpec((tm, D), lambda i: (i, 0)),
                  pl.BlockSpec((1, 1, D), lambda i: (row(i), 0, 5))],
        out_specs=pl.BlockSpec((tm, D), lambda i: (i, 0)),
        out_shape=jax.ShapeDtypeStruct((M, D), f32),
        compiler_params=_params("arbitrary"),
        name="outproj_even",
    )(y, w, x, mods)


def _outproj_odd_kernel(o_ref, hf_ref, hb_ref, gr_ref, wa_ref, wb_ref, x_ref, gt_ref, out_ref):
    r = ((hf_ref[...] + hb_ref[...]) * _gelu_tanh(gr_ref[...])).astype(bf16)
    y = _dot(o_ref[...], wa_ref[...]) + _dot(r, wb_ref[...])
    out_ref[...] = x_ref[...] + gt_ref[0] * y


def _outproj_odd(o, hf, hb, z, gr_block, w, x, mods, T):
    M, D = x.shape
    Ka, Kb = o.shape[1], hf.shape[1]
    tm = _divisor(T, 256)
    nt = T // tm
    return pl.pallas_call(
        _outproj_odd_kernel,
        grid=(M // tm,),
        in_specs=[pl.BlockSpec((tm, Ka), lambda i: (i, 0)),
                  pl.BlockSpec((tm, Kb), lambda i: (i, 0)),
                  pl.BlockSpec((tm, Kb), lambda i: (i, 0)),
                  pl.BlockSpec((tm, Kb), lambda i: (i, gr_block)),
                  pl.BlockSpec((Ka, D), lambda i: (0, 0)),
                  pl.BlockSpec((Kb, D), lambda i: (Ka // Kb, 0)),
                  pl.BlockSpec((tm, D), lambda i: (i, 0)),
                  pl.BlockSpec((1, 1, D), lambda i: (i // nt, 0, 5))],
        out_specs=pl.BlockSpec((tm, D), lambda i: (i, 0)),
        out_shape=jax.ShapeDtypeStruct((M, D), f32),
        compiler_params=_params("arbitrary"),
        name="outproj_odd",
    )(o, hf, hb, z, w, w, x, mods)


def _attn_kernel(q_ref, k_ref, v_ref, kc_ref, vc_ref, qg_ref, kg_ref, bias_ref, o_ref,
                 qn_ref, kn_ref, vn_ref, kcn_ref, vcn_ref, *, rows, kr, hd, scale):
    rows_per_trip = _divisor(rows, 8)
    lanes = 2 * hd
    W = GRID_W
    ri = jnp.where(lax.broadcasted_iota(jnp.int32, (lanes, lanes), 0) < hd, 1.0, -1.0)
    ci = jnp.where(lax.broadcasted_iota(jnp.int32, (lanes, lanes), 1) < hd, 1.0, -1.0)
    seg_mean = jnp.where(ri * ci > 0.0, 1.0 / hd, 0.0).astype(bf16)

    def head_norm(x, g):
        sq = x * x
        hi = sq.astype(bf16)
        lo = (sq - hi.astype(f32)).astype(bf16)
        ms = _dot(hi, seg_mean) + _dot(lo, seg_mean)
        return x * lax.rsqrt(ms + EPS) * g

    head0 = lax.broadcasted_iota(jnp.int32, (1, lanes), 1) < hd
    qn = head_norm(q_ref[0], qg_ref[...]) * scale
    qn_ref[0] = jnp.where(head0, qn, 0.0).astype(bf16)
    qn_ref[1] = jnp.where(head0, 0.0, qn).astype(bf16)
    kn_ref[...] = head_norm(k_ref[0], kg_ref[...]).astype(bf16)
    kcn_ref[...] = head_norm(kc_ref[0], kg_ref[...]).astype(bf16)
    vn_ref[...] = v_ref[0].astype(bf16)
    vcn_ref[...] = vc_ref[0].astype(bf16)
    nt_dims = (((1,), (1,)), ((), ()))

    def body(it, carry):
        plan = []
        for sub in range(rows_per_trip):
            r = it * rows_per_trip + sub
            r0 = jnp.clip(r - kr // 2, 0, rows - kr)
            plan.append((r - r0, pl.multiple_of(r * W, W), pl.multiple_of(r0 * W, W)))
        scores = []
        for oi, qs, ks in plan:
            kw = kn_ref[pl.ds(ks, kr * W), :]
            for hh in range(2):
                qm = qn_ref[hh, pl.ds(qs, W), :]
                sl = lax.dot_general(qm, kw, nt_dims, preferred_element_type=f32) + bias_ref[0, hh, oi]
                sc = lax.dot_general(qm, kcn_ref[...], nt_dims, preferred_element_type=f32)
                scores.append((sl, sc))
        probs = []
        for sl, sc in scores:
            m = jnp.maximum(jnp.max(sl, axis=-1, keepdims=True), jnp.max(sc, axis=-1, keepdims=True))
            el = jnp.exp(sl - m)
            ec = jnp.exp(sc - m)
            den = jnp.sum(el, axis=-1, keepdims=True) + jnp.sum(ec, axis=-1, keepdims=True)
            probs.append((el.astype(bf16), ec.astype(bf16), den))
        for n, (oi, qs, ks) in enumerate(plan):
            vw = vn_ref[pl.ds(ks, kr * W), :]
            outs = []
            for hh in range(2):
                el, ec, den = probs[2 * n + hh]
                outs.append((_dot(el, vw) + _dot(ec, vcn_ref[...])) / den)
            o_ref[0, pl.ds(qs, W), :] = jnp.where(head0, outs[0], outs[1]).astype(o_ref.dtype)
        return carry

    lax.fori_loop(0, rows // rows_per_trip, body, 0)


def _bias_table(rpb, rows):
    H = rpb.shape[0]
    R = (rpb.shape[1] + 1) // 2
    Wc = (rpb.shape[2] + 1) // 2
    kr = min(R, rows)
    col = jnp.arange(GRID_W)
    col_start = jnp.clip(col - Wc // 2, 0, GRID_W - Wc)
    col_in = (col[None, :] >= col_start[:, None]) & (col[None, :] < col_start[:, None] + Wc)
    dcol = jnp.clip(col[None, :] - col[:, None] + Wc - 1, 0, 2 * Wc - 2)
    bias_cols = jnp.where(col_in, rpb.astype(f32)[:, :, dcol], NEG_INF)
    drow = jnp.arange(kr)[None, :] - jnp.arange(kr)[:, None] + R - 1
    tab = bias_cols[:, drow]
    return jnp.moveaxis(tab, 2, 3).reshape(H, kr, GRID_W, kr * GRID_W), kr


def _attention(z, zc, qg, kg, rpb, B, L, Cn, n_heads, hd):
    rows = L // GRID_W
    bias, kr = _bias_table(rpb, rows)
    npair = n_heads // 2
    lanes = 2 * hd
    bias = bias.reshape(npair, 2, kr, GRID_W, kr * GRID_W)
    qg2 = jnp.tile(qg, 2).reshape(1, lanes)
    kg2 = jnp.tile(kg, 2).reshape(1, lanes)
    seg = lambda s: pl.BlockSpec((1, L, lanes), lambda p, b: (b, 0, s * npair + p))
    segc = lambda s: pl.BlockSpec((1, Cn, lanes), lambda p, b: (b, 0, s * npair + p))
    return pl.pallas_call(
        functools.partial(_attn_kernel, rows=rows, kr=kr, hd=hd, scale=hd ** -0.5),
        grid=(npair, B),
        in_specs=[seg(0), seg(1), seg(2), segc(0), segc(1),
                  pl.BlockSpec((1, lanes), lambda p, b: (0, 0)),
                  pl.BlockSpec((1, lanes), lambda p, b: (0, 0)),
                  pl.BlockSpec((1, 2, kr, GRID_W, kr * GRID_W), lambda p, b: (p, 0, 0, 0, 0))],
        out_specs=pl.BlockSpec((1, L, lanes), lambda p, b: (b, 0, p)),
        out_shape=jax.ShapeDtypeStruct((B, L, npair * lanes), bf16),
        scratch_shapes=[pltpu.VMEM((2, L, lanes), bf16), pltpu.VMEM((L, lanes), bf16),
                        pltpu.VMEM((L, lanes), bf16), pltpu.VMEM((Cn, lanes), bf16),
                        pltpu.VMEM((Cn, lanes), bf16)],
        compiler_params=_params("arbitrary", "arbitrary"),
        name="attention",
    )(z, z, z, zc, zc, qg2, kg2, bias)


def _lru_kernel(xfp_ref, xf_ref, xfn_ref, xbp_ref, xb_ref, xbn_ref, h0f_ref, h0b_ref, cw_ref, cb_ref,
                wbd_ref, gb_ref, lam_ref, hf_ref, hb_ref, sf_ref, sb_ref,
                xw_ref, xc_ref, a_ref, u_ref, st_ref, *, tt, tc, nj, chunk):
    B, C1 = xf_ref.shape[1], xf_ref.shape[2]
    kw = cw_ref.shape[0]
    lo = (kw - 1) // 2
    hi = kw - 1 - lo
    j = pl.program_id(0)

    @pl.when(j == 0)
    def _():
        st_ref[0] = h0f_ref[...]
        st_ref[1] = h0b_ref[...]

    for d, (xp_ref, xm_ref, xn_ref, jj) in enumerate(((xfp_ref, xf_ref, xfn_ref, j),
                                                     (xbp_ref, xb_ref, xbn_ref, nj - 1 - j))):
        xw_ref[d, 0:lo] = jnp.where(jj == 0, 0.0, xp_ref[...])
        xw_ref[d, lo:lo + tt] = xm_ref[...]
        xw_ref[d, lo + tt:lo + tt + hi] = jnp.where(jj == nj - 1, 0.0, xn_ref[...])

        def conv_body(ci, carry, d=d):
            t0 = ci * tc
            acc = cw_ref[0][None] * xw_ref[d, pl.ds(t0, tc)]
            for k in range(1, kw):
                acc = acc + cw_ref[k][None] * xw_ref[d, pl.ds(t0 + k, tc)]
            xc_ref[d, pl.ds(t0, tc)] = acc + cb_ref[...][None]
            return carry

        lax.fori_loop(0, tt // tc, conv_body, 0)

        lam = lam_ref[d]
        neg = -lam
        softplus = jnp.maximum(neg, 0.0) + jnp.log(1.0 + jnp.exp(-jnp.abs(neg)))
        rate = -LRU_C * softplus
        for c in range(C1 // chunk):
            cs = slice(c * chunk, (c + 1) * chunk)
            xs = xc_ref[d, :, :, cs].reshape(tt * B, chunk)
            xh = xs.astype(bf16)
            r = jax.nn.sigmoid(_dot(xh, wbd_ref[d, 0, c]) + gb_ref[d, 0][:, cs])
            i = jax.nn.sigmoid(_dot(xh, wbd_ref[d, 1, c]) + gb_ref[d, 1][:, cs])
            log_a = rate[:, cs] * r
            a = jnp.exp(log_a)
            mult = jnp.sqrt(jnp.tanh(-log_a) * (a * a + 1.0))
            a_ref[d, :, :, cs] = a.reshape(tt, B, chunk)
            u_ref[d, :, :, cs] = (mult * i * xs).reshape(tt, B, chunk)

    def scan_body(s, carry):
        hf, hb = carry
        tb = tt - 1 - s
        hf = a_ref[0, s] * hf + u_ref[0, s]
        hb = a_ref[1, tb] * hb + u_ref[1, tb]
        hf_ref[s] = hf
        hb_ref[tb] = hb
        return hf, hb

    hf, hb = lax.fori_loop(0, tt, scan_body, (st_ref[0], st_ref[1]))
    st_ref[0] = hf
    st_ref[1] = hb
    sf_ref[...] = hf
    sb_ref[...] = hb


def _lru(x, h0f, h0b, cw, cb, wbd, gb, lam):
    T, B, C1 = x.shape
    kw = cw.shape[0]
    lo = (kw - 1) // 2
    hi = kw - 1 - lo
    assert lo == 1 and hi == 2
    chunk = wbd.shape[-1]
    tt = _divisor(T, 64)
    assert tt % hi == 0
    nj = T // tt
    full = lambda a: pl.BlockSpec(a.shape, lambda j: (0,) * a.ndim)
    state = jax.ShapeDtypeStruct((B, C1), f32)
    seq = jax.ShapeDtypeStruct((T, B, C1), f32)

    def windows(chunk_of):
        return [pl.BlockSpec((lo, B, C1), lambda j: (jnp.maximum(chunk_of(j) * tt - 1, 0), 0, 0)),
                pl.BlockSpec((tt, B, C1), lambda j: (chunk_of(j), 0, 0)),
                pl.BlockSpec((hi, B, C1),
                             lambda j: (jnp.minimum((chunk_of(j) + 1) * (tt // hi), T // hi - 1), 0, 0))]

    fwd = lambda j: j
    bwd = lambda j: nj - 1 - j
    return pl.pallas_call(
        functools.partial(_lru_kernel, tt=tt, tc=8, nj=nj, chunk=chunk),
        grid=(nj,),
        in_specs=windows(fwd) + windows(bwd) + [full(h0f), full(h0b), full(cw), full(cb), full(wbd),
                                                full(gb), full(lam)],
        out_specs=[pl.BlockSpec((tt, B, C1), lambda j: (j, 0, 0)),
                   pl.BlockSpec((tt, B, C1), lambda j: (nj - 1 - j, 0, 0)),
                   pl.BlockSpec((B, C1), lambda j: (0, 0)),
                   pl.BlockSpec((B, C1), lambda j: (0, 0))],
        out_shape=[seq, seq, state, state],
        scratch_shapes=[pltpu.VMEM((2, tt + kw - 1, B, C1), f32), pltpu.VMEM((2, tt, B, C1), f32),
                        pltpu.VMEM((2, tt, B, C1), f32), pltpu.VMEM((2, tt, B, C1), f32),
                        pltpu.VMEM((2, B, C1), f32)],
        compiler_params=_params("arbitrary"),
        name="rglru",
    )(x, x, x, x, x, x, h0f, h0b, cw, cb, wbd, gb, lam)


def _block_diag_chunks(w, chunk):
    *lead, nb, bs, _ = w.shape
    per = chunk // bs
    w6 = w.reshape(*lead, nb // per, per, bs, bs)
    eye = jnp.eye(per, dtype=w.dtype)
    dense = jnp.einsum('...ckij,kl->...ckilj', w6, eye)
    return dense.reshape(*lead, nb // per, chunk, chunk)


def _to_tm(a, B, T):
    return jnp.swapaxes(a.reshape(B, T, a.shape[-1]), 0, 1)


def _to_bm(a):
    T, B, C = a.shape
    return jnp.swapaxes(a, 0, 1).reshape(B * T, C)


def _even_mixer(xs, g, mods, Ts, rows_const, w_in, sc_w, sc_b, cc_w, cc_b, ln_g, ln_b, w_out, B):
    D = w_in.shape[0]
    S = sc_w.shape[1]
    ts = _divisor(S, MXU_DIM)
    w_r = w_in.reshape(D, 5, S // ts, ts).transpose(0, 2, 1, 3).reshape(D, 5 * S).astype(bf16)
    w_o = w_out.astype(bf16)
    bcast = lambda a: jnp.broadcast_to(a[..., None, :], a.shape[:-1] + (B, a.shape[-1]))
    scw, scb, ccw, ccb = bcast(sc_w), bcast(sc_b), bcast(cc_w), bcast(cc_b)
    outs = []
    for x, T, rc in zip(xs, Ts, rows_const):
        bg, p, q = _evenin(x, g, mods, w_r, ts, T, rc)
        y = _even_tm(_to_tm(bg, B, T), _to_tm(p, B, T), _to_tm(q, B, T), scw, scb, ccw, ccb,
                     ln_g.reshape(1, S), ln_b.reshape(1, S))
        outs.append(_outproj_even(_to_bm(y).astype(bf16), w_o, x, mods, T, rc))
    return outs


def _odd_mixer_lat(xl, xc, g, mods, B, L, Cn, w_in, q_g, k_g, rpb, conv_w, conv_b, gate_w, gate_b, lam, w_out):
    D = w_in.shape[0]
    hd = q_g.shape[0]
    n_heads = rpb.shape[0]
    Dn = n_heads * hd
    C1 = conv_w.shape[1]
    w = w_in.astype(bf16)
    z = _modmm(xl, g, mods, w, L)
    zc = _modmm(xc, g, mods, w[:, Dn:3 * Dn + C1], Cn, row_const=B)
    o = _attention(z.reshape(B, L, -1), zc.reshape(B, Cn, -1), q_g, k_g, rpb, B, L, Cn, n_heads, hd)
    chunk = min(MXU_DIM, C1)
    wbd = _block_diag_chunks(gate_w, chunk).astype(bf16)
    bcast = lambda a: jnp.broadcast_to(a[..., None, :], a.shape[:-1] + (B, a.shape[-1]))
    cw, cb = bcast(conv_w), bcast(conv_b)
    gb = gate_b[:, :, None, :]
    lam3 = lam[:, None, :]
    zero = jnp.zeros((B, C1), f32)
    xrc = _to_tm(zc[:, 2 * Dn:2 * Dn + C1], B, Cn)
    _, _, sf, sb = _lru(xrc, zero, zero, cw, cb, wbd, gb, lam3)
    xr = _to_tm(z[:, 3 * Dn:3 * Dn + C1], B, L)
    hf, hb, _, _ = _lru(xr, sf, sb, cw, cb, wbd, gb, lam3)
    assert Dn == C1
    return _outproj_odd(o.reshape(B * L, Dn), _to_bm(hf), _to_bm(hb), z, (3 * Dn + C1) // C1,
                        w_out.astype(bf16), xl, mods, L)


def kernel(x, c, ctx, c_ctx, w_mod, b_mod, norm_g, ffn_w_in, ffn_w_out, ev_w_in, sc_w, sc_b, cc_w, cc_b, cc_ln_g, cc_ln_b, ev_w_out, od_w_in, q_norm_g, k_norm_g, na_rpb, lru_conv_w, lru_conv_b, lru_gate_w, lru_gate_b, lru_lam, od_w_out):
    B, L, D = x.shape
    Cn = ctx.shape[1]
    depth = w_mod.shape[0]
    R = -(-(B + 1) // 8) * 8
    cvec = jnp.zeros((R, D), f32).at[:B].set(c).at[B].set(c_ctx)
    mods_all = _mod_table(cvec, w_mod, b_mod)
    xl = x.reshape(B * L, D)
    xc = ctx.reshape(B * Cn, D)
    for l in range(depth):
        last = l == depth - 1
        odd = l % 2 == 1
        j = l // 2
        ctx_in = odd or not last
        ctx_out = not last
        mods = mods_all[l].reshape(R, 1, N_MOD * D)
        g = norm_g[l].reshape(3, 1, D)
        wi = ffn_w_in[l].astype(bf16)
        wo = ffn_w_out[l].astype(bf16)
        xl = _ffn(xl, g[0], mods, 0, wi[0], wo[0], L)
        if ctx_in:
            xc = _ffn(xc, g[0], mods, 0, wi[0], wo[0], Cn, row_const=B)
        if odd:
            assert ctx_in and not ctx_out
            xl = _odd_mixer_lat(xl, xc, g[1], mods, B, L, Cn, od_w_in[j], q_norm_g[j], k_norm_g[j], na_rpb[j],
                                lru_conv_w[j], lru_conv_b[j], lru_gate_w[j], lru_gate_b[j], lru_lam[j],
                                od_w_out[j])
        else:
            streams = [(xl, L, None)] + ([(xc, Cn, B)] if ctx_out else [])
            outs = _even_mixer([s[0] for s in streams], g[1], mods, [s[1] for s in streams],
                               [s[2] for s in streams], ev_w_in[j], sc_w[j], sc_b[j], cc_w[j], cc_b[j],
                               cc_ln_g[j], cc_ln_b[j], ev_w_out[j], B)
            xl = outs[0]
            if ctx_out:
                xc = outs[1]
        xl = _ffn(xl, g[2], mods, 6, wi[1], wo[1], L)
        if ctx_out:
            xc = _ffn(xc, g[2], mods, 6, wi[1], wo[1], Cn, row_const=B)
    return xl.reshape(B, L, D)
```

```python
import functools

import jax
import jax.numpy as jnp
from jax import lax
from jax.experimental import pallas as pl
from jax.experimental.pallas import tpu as pltpu

f32 = jnp.float32
bf16 = jnp.bfloat16

GRID_W = 64
LRU_C = 8.0
EPS = 1e-6
NEG_INF = -1e30
N_MOD = 9
MXU_DIM = 256
VMEM_LIMIT_BYTES = 56 * 1024 * 1024


def _divisor(n, pref):
    d = min(n, pref)
    while n % d:
        d -= 1
    return d


def _params(*sem):
    return pltpu.CompilerParams(dimension_semantics=sem, vmem_limit_bytes=VMEM_LIMIT_BYTES)


def _row_tiling(M, T, row_const, pref):
    if row_const is not None:
        return _divisor(M, pref), (lambda i: row_const)
    tm = _divisor(T, pref)
    nt = T // tm
    return tm, (lambda i: i // nt)


def _silu(x):
    return x * jax.nn.sigmoid(x)


def _gelu_tanh(x):
    return 0.5 * x * (1.0 + jnp.tanh(0.7978845608028654 * (x + 0.044715 * (x * x * x))))


def _modulate(x, g, shift, scale):
    ms = jnp.mean(x * x, axis=-1, keepdims=True)
    y = x * lax.rsqrt(ms + EPS) * g
    return y * (1.0 + scale) + shift


def _dot(a, b):
    return jnp.dot(a, b, preferred_element_type=f32)


def _mod_spec(D, row, k):
    return pl.BlockSpec((1, 1, D), lambda i, j: (row(i), 0, k))


def _mod_kernel(c_ref, w_ref, b_ref, o_ref):
    s = _silu(c_ref[...]).astype(bf16)
    o_ref[0] = _dot(s, w_ref[0].astype(bf16)) + b_ref[0]


def _mod_table(cvec, w_mod, b_mod):
    depth, D, N = w_mod.shape
    R = cvec.shape[0]
    tn = _divisor(N, 1024)
    return pl.pallas_call(
        _mod_kernel,
        grid=(depth, N // tn),
        in_specs=[pl.BlockSpec((R, D), lambda l, n: (0, 0)),
                  pl.BlockSpec((1, D, tn), lambda l, n: (l, 0, n)),
                  pl.BlockSpec((1, 1, tn), lambda l, n: (l, 0, n))],
        out_specs=pl.BlockSpec((1, R, tn), lambda l, n: (l, 0, n)),
        out_shape=jax.ShapeDtypeStruct((depth, R, N), f32),
        compiler_params=_params("arbitrary", "arbitrary"),
        name="mod_table",
    )(cvec, w_mod, b_mod.reshape(depth, 1, N))


def _ffn_kernel(x_ref, g_ref, sh_ref, sc_ref, gt_ref, wg_ref, wu_ref, wo_ref, o_ref, hm_ref, *, nf):
    f = pl.program_id(1)

    @pl.when(f == 0)
    def _():
        hm_ref[...] = _modulate(x_ref[...], g_ref[...], sh_ref[0], sc_ref[0]).astype(bf16)
        o_ref[...] = jnp.zeros_like(o_ref)

    h = hm_ref[...]
    gate = _dot(h, wg_ref[...])
    up = _dot(h, wu_ref[...])
    o_ref[...] += _dot((_silu(gate) * up).astype(bf16), wo_ref[...])

    @pl.when(f == nf - 1)
    def _():
        o_ref[...] = x_ref[...] + 0.5 * gt_ref[0] * o_ref[...]


def _ffn(x, g, mods, k0, w_in, w_out, T, row_const=None):
    M, D = x.shape
    F = w_out.shape[0]
    tm, row = _row_tiling(M, T, row_const, 512)
    tf = _divisor(F, 512)
    nf = F // tf
    return pl.pallas_call(
        functools.partial(_ffn_kernel, nf=nf),
        grid=(M // tm, nf),
        in_specs=[pl.BlockSpec((tm, D), lambda i, f: (i, 0)),
                  pl.BlockSpec((1, D), lambda i, f: (0, 0)),
                  _mod_spec(D, row, k0), _mod_spec(D, row, k0 + 1), _mod_spec(D, row, k0 + 2),
                  pl.BlockSpec((D, tf), lambda i, f: (0, f)),
                  pl.BlockSpec((D, tf), lambda i, f: (0, nf + f)),
                  pl.BlockSpec((tf, D), lambda i, f: (f, 0))],
        out_specs=pl.BlockSpec((tm, D), lambda i, f: (i, 0)),
        out_shape=jax.ShapeDtypeStruct((M, D), f32),
        scratch_shapes=[pltpu.VMEM((tm, D), bf16)],
        compiler_params=_params("arbitrary", "arbitrary"),
        name="ffn",
    )(x, g, mods, mods, mods, w_in, w_in, w_out)


def _modulate_once(x_ref, g_ref, sh_ref, sc_ref, hm_ref):
    @pl.when(pl.program_id(1) == 0)
    def _():
        hm_ref[...] = _modulate(x_ref[...], g_ref[...], sh_ref[0], sc_ref[0]).astype(bf16)


def _modmm_kernel(x_ref, g_ref, sh_ref, sc_ref, w_ref, o_ref, hm_ref):
    _modulate_once(x_ref, g_ref, sh_ref, sc_ref, hm_ref)
    o_ref[...] = _dot(hm_ref[...], w_ref[...])


def _inproj_specs(D, tm, wn, row):
    return [pl.BlockSpec((tm, D), lambda i, n: (i, 0)),
            pl.BlockSpec((1, D), lambda i, n: (0, 0)),
            _mod_spec(D, row, 3), _mod_spec(D, row, 4),
            pl.BlockSpec((D, wn), lambda i, n: (0, n))]


def _modmm(x, g, mods, w, T, row_const=None):
    M, D = x.shape
    N = w.shape[1]
    tm, row = _row_tiling(M, T, row_const, 1024)
    tn = _divisor(N, 1024)
    return pl.pallas_call(
        _modmm_kernel,
        grid=(M // tm, N // tn),
        in_specs=_inproj_specs(D, tm, tn, row),
        out_specs=pl.BlockSpec((tm, tn), lambda i, n: (i, n)),
        out_shape=jax.ShapeDtypeStruct((M, N), f32),
        scratch_shapes=[pltpu.VMEM((tm, D), bf16)],
        compiler_params=_params("arbitrary", "arbitrary"),
        name="modmm",
    )(x, g, mods, mods, w)


def _evenin_kernel(x_ref, g_ref, sh_ref, sc_ref, w_ref, bg_ref, p_ref, q_ref, hm_ref, *, ts):
    _modulate_once(x_ref, g_ref, sh_ref, sc_ref, hm_ref)
    z = _dot(hm_ref[...], w_ref[...])
    bg_ref[...] = z[:, 0:ts]
    p_ref[...] = z[:, ts:2 * ts] * z[:, 2 * ts:3 * ts]
    q_ref[...] = z[:, 3 * ts:4 * ts] * jax.nn.sigmoid(z[:, 4 * ts:5 * ts])


def _evenin(x, g, mods, w_r, ts, T, row_const=None):
    M, D = x.shape
    S = w_r.shape[1] // 5
    tm, row = _row_tiling(M, T, row_const, 1024)
    out = jax.ShapeDtypeStruct((M, S), f32)
    ospec = pl.BlockSpec((tm, ts), lambda i, n: (i, n))
    return pl.pallas_call(
        functools.partial(_evenin_kernel, ts=ts),
        grid=(M // tm, S // ts),
        in_specs=_inproj_specs(D, tm, 5 * ts, row),
        out_specs=[ospec, ospec, ospec],
        out_shape=[out, out, out],
        scratch_shapes=[pltpu.VMEM((tm, D), bf16)],
        compiler_params=_params("arbitrary", "arbitrary"),
        name="evenin",
    )(x, g, mods, mods, w_r)


def _shifted_taps(win, w_ref, cs, lo, halo, tm):
    acc = None
    for s in range(8):
        taps = [k for k in range(w_ref.shape[0]) if (halo + k - lo + s) % 8 == 0]
        if not taps:
            continue
        wr = win if s == 0 else pltpu.roll(win, s, axis=0)
        for k in taps:
            st = halo + k - lo + s
            term = w_ref[k:k + 1, cs] * wr[st:st + tm]
            acc = term if acc is None else acc + term
    return acc


def _even_out_kernel(bg_ref, pp_ref, p_ref, pn_ref, qp_ref, q_ref, qn_ref, scw_ref, scb_ref, ccw_ref, ccb_ref,
                     lng_ref, lnb_ref, w_ref, x_ref, gt_ref, o_ref, ysc_ref, u_ref, *, tm):
    S = bg_ref.shape[1]
    hp, hq = pp_ref.shape[0], qp_ref.shape[0]
    t = pl.program_id(1)
    first = t == 0
    last = t == pl.num_programs(1) - 1
    lo_sc = (scw_ref.shape[0] - 1) // 2
    lo_cc = (ccw_ref.shape[0] - 1) // 2
    for c in range(S // 128):
        cs = slice(c * 128, (c + 1) * 128)
        pw = jnp.concatenate([jnp.where(first, 0.0, pp_ref[:, cs]), p_ref[:, cs],
                              jnp.where(last, 0.0, pn_ref[:, cs])], axis=0)
        conv = _shifted_taps(pw, scw_ref, cs, lo_sc, hp, tm)
        ysc_ref[:, cs] = (bg_ref[:, cs] * (conv + scb_ref[:, cs])).astype(bf16)
    y = _dot(ysc_ref[...], w_ref[0:S, :])
    for c in range(S // 128):
        cs = slice(c * 128, (c + 1) * 128)
        qw = jnp.concatenate([jnp.where(first, 0.0, qp_ref[:, cs]), q_ref[:, cs],
                              jnp.where(last, 0.0, qn_ref[:, cs])], axis=0)
        u_ref[:, cs] = _shifted_taps(qw, ccw_ref, cs, lo_cc, hq, tm) + ccb_ref[:, cs]
    u = u_ref[...]
    mu = jnp.mean(u, axis=-1, keepdims=True)
    xc = u - mu
    var = jnp.mean(xc * xc, axis=-1, keepdims=True)
    ycc = _silu(xc * lax.rsqrt(var + EPS) * lng_ref[...] + lnb_ref[...]).astype(bf16)
    y = y + _dot(ycc, w_ref[S:2 * S, :])
    o_ref[...] = x_ref[...] + gt_ref[0] * y


def _even_out(bg, p, q, scw, scb, ccw, ccb, lng, lnb, w, x, mods, B, T, row_const=None):
    M, D = x.shape
    S = bg.shape[1]
    hp, hq = 8, 16
    assert (scw.shape[0] - 1) // 2 <= hp and (ccw.shape[0] - 1) // 2 < hq
    tm = _divisor(T, 256)
    nt = T // tm
    rp, rq = tm // hp, tm // hq
    tile = lambda b, t: b * nt + t
    cur = pl.BlockSpec((tm, S), lambda b, t: (tile(b, t), 0))
    prev = lambda h, r: pl.BlockSpec((h, S), lambda b, t: (jnp.maximum(tile(b, t) * r - 1, 0), 0))
    nxt = lambda h, r: pl.BlockSpec((h, S), lambda b, t: (jnp.minimum((tile(b, t) + 1) * r, M // h - 1), 0))
    full = lambda a: pl.BlockSpec(a.shape, lambda b, t: (0,) * a.ndim)
    row = (lambda b: row_const) if row_const is not None else (lambda b: b)
    return pl.pallas_call(
        functools.partial(_even_out_kernel, tm=tm),
        grid=(B, nt),
        in_specs=[cur, prev(hp, rp), cur, nxt(hp, rp), prev(hq, rq), cur, nxt(hq, rq),
                  full(scw), full(scb), full(ccw), full(ccb), full(lng), full(lnb), full(w),
                  pl.BlockSpec((tm, D), lambda b, t: (tile(b, t), 0)),
                  pl.BlockSpec((1, 1, D), lambda b, t: (row(b), 0, 5))],
        out_specs=pl.BlockSpec((tm, D), lambda b, t: (tile(b, t), 0)),
        out_shape=jax.ShapeDtypeStruct((M, D), f32),
        scratch_shapes=[pltpu.VMEM((tm, S), bf16), pltpu.VMEM((tm, S), f32)],
        compiler_params=_params("arbitrary", "arbitrary"),
        name="even_out",
    )(bg, p, p, p, q, q, q, scw, scb, ccw, ccb, lng, lnb, w, x, mods)


def _outproj_odd_kernel(o_ref, hf_ref, hb_ref, gr_ref, wa_ref, wb_ref, x_ref, gt_ref, out_ref):
    r = ((hf_ref[...] + hb_ref[...]) * _gelu_tanh(gr_ref[...])).astype(bf16)
    y = _dot(o_ref[...], wa_ref[...]) + _dot(r, wb_ref[...])
    out_ref[...] = x_ref[...] + gt_ref[0] * y


def _outproj_odd(o, hf, hb, z, gr_block, w, x, mods, T):
    M, D = x.shape
    Ka, Kb = o.shape[1], hf.shape[1]
    tm = _divisor(T, 256)
    nt = T // tm
    return pl.pallas_call(
        _outproj_odd_kernel,
        grid=(M // tm,),
        in_specs=[pl.BlockSpec((tm, Ka), lambda i: (i, 0)),
                  pl.BlockSpec((tm, Kb), lambda i: (i, 0)),
                  pl.BlockSpec((tm, Kb), lambda i: (i, 0)),
                  pl.BlockSpec((tm, Kb), lambda i: (i, gr_block)),
                  pl.BlockSpec((Ka, D), lambda i: (0, 0)),
                  pl.BlockSpec((Kb, D), lambda i: (Ka // Kb, 0)),
                  pl.BlockSpec((tm, D), lambda i: (i, 0)),
                  pl.BlockSpec((1, 1, D), lambda i: (i // nt, 0, 5))],
        out_specs=pl.BlockSpec((tm, D), lambda i: (i, 0)),
        out_shape=jax.ShapeDtypeStruct((M, D), f32),
        compiler_params=_params("arbitrary"),
        name="outproj_odd",
    )(o, hf, hb, z, w, w, x, mods)


def _attn_kernel(q_ref, k_ref, v_ref, kc_ref, vc_ref, qg_ref, kg_ref, bias_ref, o_ref,
                 qn_ref, kn_ref, vn_ref, kcn_ref, vcn_ref, *, rows, kr, hd, scale):
    rows_per_trip = _divisor(rows, 8)
    lanes = 2 * hd
    W = GRID_W
    ri = jnp.where(lax.broadcasted_iota(jnp.int32, (lanes, lanes), 0) < hd, 1.0, -1.0)
    ci = jnp.where(lax.broadcasted_iota(jnp.int32, (lanes, lanes), 1) < hd, 1.0, -1.0)
    seg_mean = jnp.where(ri * ci > 0.0, 1.0 / hd, 0.0).astype(bf16)

    def head_norm(x, g):
        sq = x * x
        hi = sq.astype(bf16)
        lo = (sq - hi.astype(f32)).astype(bf16)
        ms = _dot(hi, seg_mean) + _dot(lo, seg_mean)
        return x * lax.rsqrt(ms + EPS) * g

    head0 = lax.broadcasted_iota(jnp.int32, (1, lanes), 1) < hd
    qn = head_norm(q_ref[0], qg_ref[...]) * scale
    qn_ref[0] = jnp.where(head0, qn, 0.0).astype(bf16)
    qn_ref[1] = jnp.where(head0, 0.0, qn).astype(bf16)
    kn_ref[...] = head_norm(k_ref[0], kg_ref[...]).astype(bf16)
    kcn_ref[...] = head_norm(kc_ref[0], kg_ref[...]).astype(bf16)
    vn_ref[...] = v_ref[0].astype(bf16)
    vcn_ref[...] = vc_ref[0].astype(bf16)
    nt_dims = (((1,), (1,)), ((), ()))

    def body(it, carry):
        plan = []
        for sub in range(rows_per_trip):
            r = it * rows_per_trip + sub
            r0 = jnp.clip(r - kr // 2, 0, rows - kr)
            plan.append((r - r0, pl.multiple_of(r * W, W), pl.multiple_of(r0 * W, W)))
        scores = []
        for oi, qs, ks in plan:
            kw = kn_ref[pl.ds(ks, kr * W), :]
            for hh in range(2):
                qm = qn_ref[hh, pl.ds(qs, W), :]
                sl = lax.dot_general(qm, kw, nt_dims, preferred_element_type=f32) + bias_ref[0, hh, oi]
                sc = lax.dot_general(qm, kcn_ref[...], nt_dims, preferred_element_type=f32)
                scores.append((sl, sc))
        probs = []
        for sl, sc in scores:
            m = jnp.maximum(jnp.max(sl, axis=-1, keepdims=True), jnp.max(sc, axis=-1, keepdims=True))
            el = jnp.exp(sl - m)
            ec = jnp.exp(sc - m)
            den = jnp.sum(el, axis=-1, keepdims=True) + jnp.sum(ec, axis=-1, keepdims=True)
            probs.append((el.astype(bf16), ec.astype(bf16), den))
        for n, (oi, qs, ks) in enumerate(plan):
            vw = vn_ref[pl.ds(ks, kr * W), :]
            outs = []
            for hh in range(2):
                el, ec, den = probs[2 * n + hh]
                outs.append((_dot(el, vw) + _dot(ec, vcn_ref[...])) / den)
            o_ref[0, pl.ds(qs, W), :] = jnp.where(head0, outs[0], outs[1]).astype(o_ref.dtype)
        return carry

    lax.fori_loop(0, rows // rows_per_trip, body, 0)


def _bias_table(rpb, rows):
    H = rpb.shape[0]
    R = (rpb.shape[1] + 1) // 2
    Wc = (rpb.shape[2] + 1) // 2
    kr = min(R, rows)
    col = jnp.arange(GRID_W)
    col_start = jnp.clip(col - Wc // 2, 0, GRID_W - Wc)
    col_in = (col[None, :] >= col_start[:, None]) & (col[None, :] < col_start[:, None] + Wc)
    dcol = jnp.clip(col[None, :] - col[:, None] + Wc - 1, 0, 2 * Wc - 2)
    bias_cols = jnp.where(col_in, rpb.astype(f32)[:, :, dcol], NEG_INF)
    drow = jnp.arange(kr)[None, :] - jnp.arange(kr)[:, None] + R - 1
    tab = bias_cols[:, drow]
    return jnp.moveaxis(tab, 2, 3).reshape(H, kr, GRID_W, kr * GRID_W), kr


def _attention(z, zc, qg, kg, rpb, B, L, Cn, n_heads, hd):
    rows = L // GRID_W
    bias, kr = _bias_table(rpb, rows)
    npair = n_heads // 2
    lanes = 2 * hd
    bias = bias.reshape(npair, 2, kr, GRID_W, kr * GRID_W)
    qg2 = jnp.tile(qg, 2).reshape(1, lanes)
    kg2 = jnp.tile(kg, 2).reshape(1, lanes)
    seg = lambda s: pl.BlockSpec((1, L, lanes), lambda p, b: (b, 0, s * npair + p))
    segc = lambda s: pl.BlockSpec((1, Cn, lanes), lambda p, b: (b, 0, s * npair + p))
    return pl.pallas_call(
        functools.partial(_attn_kernel, rows=rows, kr=kr, hd=hd, scale=hd ** -0.5),
        grid=(npair, B),
        in_specs=[seg(0), seg(1), seg(2), segc(0), segc(1),
                  pl.BlockSpec((1, lanes), lambda p, b: (0, 0)),
                  pl.BlockSpec((1, lanes), lambda p, b: (0, 0)),
                  pl.BlockSpec((1, 2, kr, GRID_W, kr * GRID_W), lambda p, b: (p, 0, 0, 0, 0))],
        out_specs=pl.BlockSpec((1, L, lanes), lambda p, b: (b, 0, p)),
        out_shape=jax.ShapeDtypeStruct((B, L, npair * lanes), bf16),
        scratch_shapes=[pltpu.VMEM((2, L, lanes), bf16), pltpu.VMEM((L, lanes), bf16),
                        pltpu.VMEM((L, lanes), bf16), pltpu.VMEM((Cn, lanes), bf16),
                        pltpu.VMEM((Cn, lanes), bf16)],
        compiler_params=_params("arbitrary", "arbitrary"),
        name="attention",
    )(z, z, z, zc, zc, qg2, kg2, bias)


def _lru_kernel(xfp_ref, xf_ref, xfn_ref, xbp_ref, xb_ref, xbn_ref, h0f_ref, h0b_ref, cw_ref, cb_ref,
                wbd_ref, gb_ref, lam_ref, hf_ref, hb_ref, sf_ref, sb_ref,
                xw_ref, xc_ref, a_ref, u_ref, st_ref, *, tt, tc, nj, chunk):
    B, C1 = xf_ref.shape[1], xf_ref.shape[2]
    kw = cw_ref.shape[0]
    lo = (kw - 1) // 2
    hi = kw - 1 - lo
    j = pl.program_id(0)

    @pl.when(j == 0)
    def _():
        st_ref[0] = h0f_ref[...]
        st_ref[1] = h0b_ref[...]

    for d, (xp_ref, xm_ref, xn_ref, jj) in enumerate(((xfp_ref, xf_ref, xfn_ref, j),
                                                     (xbp_ref, xb_ref, xbn_ref, nj - 1 - j))):
        xw_ref[d, 0:lo] = jnp.where(jj == 0, 0.0, xp_ref[...])
        xw_ref[d, lo:lo + tt] = xm_ref[...]
        xw_ref[d, lo + tt:lo + tt + hi] = jnp.where(jj == nj - 1, 0.0, xn_ref[...])

        def conv_body(ci, carry, d=d):
            t0 = ci * tc
            acc = cw_ref[0][None] * xw_ref[d, pl.ds(t0, tc)]
            for k in range(1, kw):
                acc = acc + cw_ref[k][None] * xw_ref[d, pl.ds(t0 + k, tc)]
            xc_ref[d, pl.ds(t0, tc)] = acc + cb_ref[...][None]
            return carry

        lax.fori_loop(0, tt // tc, conv_body, 0)

        lam = lam_ref[d]
        neg = -lam
        softplus = jnp.maximum(neg, 0.0) + jnp.log(1.0 + jnp.exp(-jnp.abs(neg)))
        rate = -LRU_C * softplus
        for c in range(C1 // chunk):
            cs = slice(c * chunk, (c + 1) * chunk)
            xs = xc_ref[d, :, :, cs].reshape(tt * B, chunk)
            xh = xs.astype(bf16)
            r = jax.nn.sigmoid(_dot(xh, wbd_ref[d, 0, c]) + gb_ref[d, 0][:, cs])
            i = jax.nn.sigmoid(_dot(xh, wbd_ref[d, 1, c]) + gb_ref[d, 1][:, cs])
            log_a = rate[:, cs] * r
            a = jnp.exp(log_a)
            mult = jnp.sqrt(jnp.tanh(-log_a) * (a * a + 1.0))
            a_ref[d, :, :, cs] = a.reshape(tt, B, chunk)
            u_ref[d, :, :, cs] = (mult * i * xs).reshape(tt, B, chunk)

    def scan_body(s, carry):
        hf, hb = carry
        tb = tt - 1 - s
        hf = a_ref[0, s] * hf + u_ref[0, s]
        hb = a_ref[1, tb] * hb + u_ref[1, tb]
        hf_ref[s] = hf
        hb_ref[tb] = hb
        return hf, hb

    hf, hb = lax.fori_loop(0, tt, scan_body, (st_ref[0], st_ref[1]))
    st_ref[0] = hf
    st_ref[1] = hb
    sf_ref[...] = hf
    sb_ref[...] = hb


def _lru(x, h0f, h0b, cw, cb, wbd, gb, lam):
    T, B, C1 = x.shape
    kw = cw.shape[0]
    lo = (kw - 1) // 2
    hi = kw - 1 - lo
    assert lo == 1 and hi == 2
    chunk = wbd.shape[-1]
    tt = _divisor(T, 64)
    assert tt % hi == 0
    nj = T // tt
    full = lambda a: pl.BlockSpec(a.shape, lambda j: (0,) * a.ndim)
    state = jax.ShapeDtypeStruct((B, C1), f32)
    seq = jax.ShapeDtypeStruct((T, B, C1), f32)

    def windows(chunk_of):
        return [pl.BlockSpec((lo, B, C1), lambda j: (jnp.maximum(chunk_of(j) * tt - 1, 0), 0, 0)),
                pl.BlockSpec((tt, B, C1), lambda j: (chunk_of(j), 0, 0)),
                pl.BlockSpec((hi, B, C1),
                             lambda j: (jnp.minimum((chunk_of(j) + 1) * (tt // hi), T // hi - 1), 0, 0))]

    fwd = lambda j: j
    bwd = lambda j: nj - 1 - j
    return pl.pallas_call(
        functools.partial(_lru_kernel, tt=tt, tc=8, nj=nj, chunk=chunk),
        grid=(nj,),
        in_specs=windows(fwd) + windows(bwd) + [full(h0f), full(h0b), full(cw), full(cb), full(wbd),
                                                full(gb), full(lam)],
        out_specs=[pl.BlockSpec((tt, B, C1), lambda j: (j, 0, 0)),
                   pl.BlockSpec((tt, B, C1), lambda j: (nj - 1 - j, 0, 0)),
                   pl.BlockSpec((B, C1), lambda j: (0, 0)),
                   pl.BlockSpec((B, C1), lambda j: (0, 0))],
        out_shape=[seq, seq, state, state],
        scratch_shapes=[pltpu.VMEM((2, tt + kw - 1, B, C1), f32), pltpu.VMEM((2, tt, B, C1), f32),
                        pltpu.VMEM((2, tt, B, C1), f32), pltpu.VMEM((2, tt, B, C1), f32),
                        pltpu.VMEM((2, B, C1), f32)],
        compiler_params=_params("arbitrary"),
        name="rglru",
    )(x, x, x, x, x, x, h0f, h0b, cw, cb, wbd, gb, lam)


def _block_diag_chunks(w, chunk):
    *lead, nb, bs, _ = w.shape
    per = chunk // bs
    w6 = w.reshape(*lead, nb // per, per, bs, bs)
    eye = jnp.eye(per, dtype=w.dtype)
    dense = jnp.einsum('...ckij,kl->...ckilj', w6, eye)
    return dense.reshape(*lead, nb // per, chunk, chunk)


def _to_tm(a, B, T):
    return jnp.swapaxes(a.reshape(B, T, a.shape[-1]), 0, 1)


def _to_bm(a):
    T, B, C = a.shape
    return jnp.swapaxes(a, 0, 1).reshape(B * T, C)


def _even_mixer(xs, g, mods, Ts, rows_const, w_in, sc_w, sc_b, cc_w, cc_b, ln_g, ln_b, w_out, B):
    D = w_in.shape[0]
    S = sc_w.shape[1]
    ts = _divisor(S, MXU_DIM)
    w_r = w_in.reshape(D, 5, S // ts, ts).transpose(0, 2, 1, 3).reshape(D, 5 * S).astype(bf16)
    w_o = w_out.astype(bf16)
    row1 = lambda a: a.reshape(1, S)
    outs = []
    for x, T, rc in zip(xs, Ts, rows_const):
        bg, p, q = _evenin(x, g, mods, w_r, ts, T, rc)
        outs.append(_even_out(bg, p, q, sc_w, row1(sc_b), cc_w, row1(cc_b), row1(ln_g), row1(ln_b), w_o, x, mods,
                              B, T, rc))
    return outs


def _odd_mixer_lat(xl, xc, g, mods, B, L, Cn, w_in, q_g, k_g, rpb, conv_w, conv_b, gate_w, gate_b, lam, w_out):
    D = w_in.shape[0]
    hd = q_g.shape[0]
    n_heads = rpb.shape[0]
    Dn = n_heads * hd
    C1 = conv_w.shape[1]
    w = w_in.astype(bf16)
    z = _modmm(xl, g, mods, w, L)
    zc = _modmm(xc, g, mods, w[:, Dn:3 * Dn + C1], Cn, row_const=B)
    o = _attention(z.reshape(B, L, -1), zc.reshape(B, Cn, -1), q_g, k_g, rpb, B, L, Cn, n_heads, hd)
    chunk = min(MXU_DIM, C1)
    wbd = _block_diag_chunks(gate_w, chunk).astype(bf16)
    bcast = lambda a: jnp.broadcast_to(a[..., None, :], a.shape[:-1] + (B, a.shape[-1]))
    cw, cb = bcast(conv_w), bcast(conv_b)
    gb = gate_b[:, :, None, :]
    lam3 = lam[:, None, :]
    zero = jnp.zeros((B, C1), f32)
    xrc = _to_tm(zc[:, 2 * Dn:2 * Dn + C1], B, Cn)
    _, _, sf, sb = _lru(xrc, zero, zero, cw, cb, wbd, gb, lam3)
    xr = _to_tm(z[:, 3 * Dn:3 * Dn + C1], B, L)
    hf, hb, _, _ = _lru(xr, sf, sb, cw, cb, wbd, gb, lam3)
    assert Dn == C1
    return _outproj_odd(o.reshape(B * L, Dn), _to_bm(hf), _to_bm(hb), z, (3 * Dn + C1) // C1,
                        w_out.astype(bf16), xl, mods, L)


def kernel(x, c, ctx, c_ctx, w_mod, b_mod, norm_g, ffn_w_in, ffn_w_out, ev_w_in, sc_w, sc_b, cc_w, cc_b, cc_ln_g, cc_ln_b, ev_w_out, od_w_in, q_norm_g, k_norm_g, na_rpb, lru_conv_w, lru_conv_b, lru_gate_w, lru_gate_b, lru_lam, od_w_out):
    B, L, D = x.shape
    Cn = ctx.shape[1]
    depth = w_mod.shape[0]
    R = -(-(B + 1) // 8) * 8
    cvec = jnp.zeros((R, D), f32).at[:B].set(c).at[B].set(c_ctx)
    mods_all = _mod_table(cvec, w_mod, b_mod)
    xl = x.reshape(B * L, D)
    xc = ctx.reshape(B * Cn, D)
    for l in range(depth):
        last = l == depth - 1
        odd = l % 2 == 1
        j = l // 2
        ctx_in = odd or not last
        ctx_out = not last
        mods = mods_all[l].reshape(R, 1, N_MOD * D)
        g = norm_g[l].reshape(3, 1, D)
        wi = ffn_w_in[l].astype(bf16)
        wo = ffn_w_out[l].astype(bf16)
        xl = _ffn(xl, g[0], mods, 0, wi[0], wo[0], L)
        if ctx_in:
            xc = _ffn(xc, g[0], mods, 0, wi[0], wo[0], Cn, row_const=B)
        if odd:
            assert ctx_in and not ctx_out
            xl = _odd_mixer_lat(xl, xc, g[1], mods, B, L, Cn, od_w_in[j], q_norm_g[j], k_norm_g[j], na_rpb[j],
                                lru_conv_w[j], lru_conv_b[j], lru_gate_w[j], lru_gate_b[j], lru_lam[j],
                                od_w_out[j])
        else:
            streams = [(xl, L, None)] + ([(xc, Cn, B)] if ctx_out else [])
            outs = _even_mixer([s[0] for s in streams], g[1], mods, [s[1] for s in streams],
                               [s[2] for s in streams], ev_w_in[j], sc_w[j], sc_b[j], cc_w[j], cc_b[j],
                               cc_ln_g[j], cc_ln_b[j], ev_w_out[j], B)
            xl = outs[0]
            if ctx_out:
                xc = outs[1]
        xl = _ffn(xl, g[2], mods, 6, wi[1], wo[1], L)
        if ctx_out:
            xc = _ffn(xc, g[2], mods, 6, wi[1], wo[1], Cn, row_const=B)
    return xl.reshape(B, L, D)
```

```python
import functools

import jax
import jax.numpy as jnp
from jax import lax
from jax.experimental import pallas as pl
from jax.experimental.pallas import tpu as pltpu

f32 = jnp.float32
bf16 = jnp.bfloat16

GRID_W = 64
LRU_C = 8.0
EPS = 1e-6
NEG_INF = -1e30
N_MOD = 9
MXU_DIM = 256
VMEM_LIMIT_BYTES = 60 * 1024 * 1024


def _divisor(n, pref):
    d = min(n, pref)
    while n % d:
        d -= 1
    return d


def _params(*sem):
    return pltpu.CompilerParams(dimension_semantics=sem, vmem_limit_bytes=VMEM_LIMIT_BYTES)


def _row_tiling(M, T, row_const, pref):
    if row_const is not None:
        return _divisor(M, pref), (lambda i: row_const)
    tm = _divisor(T, pref)
    nt = T // tm
    return tm, (lambda i: i // nt)


def _silu(x):
    return x * jax.nn.sigmoid(x)


def _gelu_tanh(x):
    return 0.5 * x * (1.0 + jnp.tanh(0.7978845608028654 * (x + 0.044715 * (x * x * x))))


MOD_ROWS = 16
MOD_CHUNKS = 8


def _modulate_rows(x_ref, g_ref, sh_ref, sc_ref, hm_ref, zero_ref=None):
    tm, D = x_ref.shape
    rc = _divisor(tm, MOD_ROWS)
    per_trip = _divisor(tm // rc, MOD_CHUNKS)
    gain = g_ref[...] * (1.0 + sc_ref[0])
    shift = sh_ref[0]

    def body(i, carry):
        for sub in range(per_trip):
            r0 = pl.multiple_of((i * per_trip + sub) * rc, rc)
            x = x_ref[pl.ds(r0, rc), :]
            ms = jnp.mean(x * x, axis=-1, keepdims=True)
            hm_ref[pl.ds(r0, rc), :] = (x * lax.rsqrt(ms + EPS) * gain + shift).astype(bf16)
            if zero_ref is not None:
                zero_ref[pl.ds(r0, rc), :] = jnp.zeros((rc, D), zero_ref.dtype)
        return carry

    lax.fori_loop(0, tm // (rc * per_trip), body, 0)


def _dot(a, b):
    return jnp.dot(a, b, preferred_element_type=f32)


def _mod_spec(D, row, k):
    return pl.BlockSpec((1, 1, D), lambda i, j: (row(i), 0, k))


def _mod_kernel(c_ref, w_ref, b_ref, o_ref):
    s = _silu(c_ref[...]).astype(bf16)
    o_ref[0] = _dot(s, w_ref[0].astype(bf16)) + b_ref[0]


def _mod_table(cvec, w_mod, b_mod):
    depth, D, N = w_mod.shape
    R = cvec.shape[0]
    tn = _divisor(N, 1024)
    return pl.pallas_call(
        _mod_kernel,
        grid=(depth, N // tn),
        in_specs=[pl.BlockSpec((R, D), lambda l, n: (0, 0)),
                  pl.BlockSpec((1, D, tn), lambda l, n: (l, 0, n)),
                  pl.BlockSpec((1, 1, tn), lambda l, n: (l, 0, n))],
        out_specs=pl.BlockSpec((1, R, tn), lambda l, n: (l, 0, n)),
        out_shape=jax.ShapeDtypeStruct((depth, R, N), f32),
        compiler_params=_params("arbitrary", "arbitrary"),
        name="mod_table",
    )(cvec, w_mod, b_mod.reshape(depth, 1, N))


def _ffn_kernel(x_ref, g_ref, sh_ref, sc_ref, gt_ref, wg_ref, wu_ref, wo_ref, o_ref, hm_ref, *, nf):
    f = pl.program_id(1)

    @pl.when(f == 0)
    def _():
        _modulate_rows(x_ref, g_ref, sh_ref, sc_ref, hm_ref, zero_ref=o_ref)

    h = hm_ref[...]
    gate = _dot(h, wg_ref[...])
    up = _dot(h, wu_ref[...])
    o_ref[...] += _dot((_silu(gate) * up).astype(bf16), wo_ref[...])

    @pl.when(f == nf - 1)
    def _():
        o_ref[...] = x_ref[...] + 0.5 * gt_ref[0] * o_ref[...]


def _ffn(x, g, mods, k0, w_in, w_out, T, row_const=None):
    M, D = x.shape
    F = w_out.shape[0]
    tm, row = _row_tiling(M, T, row_const, 1024)
    tf = _divisor(F, 512)
    nf = F // tf
    return pl.pallas_call(
        functools.partial(_ffn_kernel, nf=nf),
        grid=(M // tm, nf),
        in_specs=[pl.BlockSpec((tm, D), lambda i, f: (i, 0)),
                  pl.BlockSpec((1, D), lambda i, f: (0, 0)),
                  _mod_spec(D, row, k0), _mod_spec(D, row, k0 + 1), _mod_spec(D, row, k0 + 2),
                  pl.BlockSpec((D, tf), lambda i, f: (0, f)),
                  pl.BlockSpec((D, tf), lambda i, f: (0, nf + f)),
                  pl.BlockSpec((tf, D), lambda i, f: (f, 0))],
        out_specs=pl.BlockSpec((tm, D), lambda i, f: (i, 0)),
        out_shape=jax.ShapeDtypeStruct((M, D), f32),
        scratch_shapes=[pltpu.VMEM((tm, D), bf16)],
        compiler_params=_params("arbitrary", "arbitrary"),
        name="ffn",
    )(x, g, mods, mods, mods, w_in, w_in, w_out)


def _modulate_once(x_ref, g_ref, sh_ref, sc_ref, hm_ref):
    @pl.when(pl.program_id(1) == 0)
    def _():
        _modulate_rows(x_ref, g_ref, sh_ref, sc_ref, hm_ref)


def _modmm_kernel(x_ref, g_ref, sh_ref, sc_ref, w_ref, o_ref, hm_ref):
    _modulate_once(x_ref, g_ref, sh_ref, sc_ref, hm_ref)
    o_ref[...] = _dot(hm_ref[...], w_ref[...])


def _inproj_specs(D, tm, wn, row):
    return [pl.BlockSpec((tm, D), lambda i, n: (i, 0)),
            pl.BlockSpec((1, D), lambda i, n: (0, 0)),
            _mod_spec(D, row, 3), _mod_spec(D, row, 4),
            pl.BlockSpec((D, wn), lambda i, n: (0, n))]


def _modmm(x, g, mods, w, T, row_const=None):
    M, D = x.shape
    N = w.shape[1]
    tm, row = _row_tiling(M, T, row_const, 1024)
    tn = _divisor(N, 1024)
    return pl.pallas_call(
        _modmm_kernel,
        grid=(M // tm, N // tn),
        in_specs=_inproj_specs(D, tm, tn, row),
        out_specs=pl.BlockSpec((tm, tn), lambda i, n: (i, n)),
        out_shape=jax.ShapeDtypeStruct((M, N), f32),
        scratch_shapes=[pltpu.VMEM((tm, D), bf16)],
        compiler_params=_params("arbitrary", "arbitrary"),
        name="modmm",
    )(x, g, mods, mods, w)


def _evenin_kernel(x_ref, g_ref, sh_ref, sc_ref, w_ref, bg_ref, p_ref, q_ref, hm_ref, *, ts):
    _modulate_once(x_ref, g_ref, sh_ref, sc_ref, hm_ref)
    z = _dot(hm_ref[...], w_ref[...])
    bg_ref[...] = z[:, 0:ts]
    p_ref[...] = z[:, ts:2 * ts] * z[:, 2 * ts:3 * ts]
    q_ref[...] = z[:, 3 * ts:4 * ts] * jax.nn.sigmoid(z[:, 4 * ts:5 * ts])


def _evenin(x, g, mods, w_r, ts, T, row_const=None):
    M, D = x.shape
    S = w_r.shape[1] // 5
    tm, row = _row_tiling(M, T, row_const, 1024)
    out = jax.ShapeDtypeStruct((M, S), f32)
    ospec = pl.BlockSpec((tm, ts), lambda i, n: (i, n))
    return pl.pallas_call(
        functools.partial(_evenin_kernel, ts=ts),
        grid=(M // tm, S // ts),
        in_specs=_inproj_specs(D, tm, 5 * ts, row),
        out_specs=[ospec, ospec, ospec],
        out_shape=[out, out, out],
        scratch_shapes=[pltpu.VMEM((tm, D), bf16)],
        compiler_params=_params("arbitrary", "arbitrary"),
        name="evenin",
    )(x, g, mods, mods, w_r)


def _shifted_taps(win, w_ref, cs, lo, halo, tm):
    acc = None
    for s in range(8):
        taps = [k for k in range(w_ref.shape[0]) if (halo + k - lo + s) % 8 == 0]
        if not taps:
            continue
        wr = win if s == 0 else pltpu.roll(win, s, axis=0)
        for k in taps:
            st = halo + k - lo + s
            term = w_ref[k:k + 1, cs] * wr[st:st + tm]
            acc = term if acc is None else acc + term
    return acc


def _even_out_kernel(bg_ref, pp_ref, p_ref, pn_ref, qp_ref, q_ref, qn_ref, scw_ref, scb_ref, ccw_ref, ccb_ref,
                     lng_ref, lnb_ref, w_ref, x_ref, gt_ref, o_ref, ysc_ref, u_ref, *, tm):
    S = bg_ref.shape[1]
    hp, hq = pp_ref.shape[0], qp_ref.shape[0]
    t = pl.program_id(1)
    first = t == 0
    last = t == pl.num_programs(1) - 1
    lo_sc = (scw_ref.shape[0] - 1) // 2
    lo_cc = (ccw_ref.shape[0] - 1) // 2
    for c in range(S // 128):
        cs = slice(c * 128, (c + 1) * 128)
        pw = jnp.concatenate([jnp.where(first, 0.0, pp_ref[:, cs]), p_ref[:, cs],
                              jnp.where(last, 0.0, pn_ref[:, cs])], axis=0)
        conv = _shifted_taps(pw, scw_ref, cs, lo_sc, hp, tm)
        ysc_ref[:, cs] = (bg_ref[:, cs] * (conv + scb_ref[:, cs])).astype(bf16)
    y = _dot(ysc_ref[...], w_ref[0:S, :])
    for c in range(S // 128):
        cs = slice(c * 128, (c + 1) * 128)
        qw = jnp.concatenate([jnp.where(first, 0.0, qp_ref[:, cs]), q_ref[:, cs],
                              jnp.where(last, 0.0, qn_ref[:, cs])], axis=0)
        u_ref[:, cs] = _shifted_taps(qw, ccw_ref, cs, lo_cc, hq, tm) + ccb_ref[:, cs]
    u = u_ref[...]
    mu = jnp.mean(u, axis=-1, keepdims=True)
    xc = u - mu
    var = jnp.mean(xc * xc, axis=-1, keepdims=True)
    ycc = _silu(xc * lax.rsqrt(var + EPS) * lng_ref[...] + lnb_ref[...]).astype(bf16)
    y = y + _dot(ycc, w_ref[S:2 * S, :])
    o_ref[...] = x_ref[...] + gt_ref[0] * y


def _even_out(bg, p, q, scw, scb, ccw, ccb, lng, lnb, w, x, mods, B, T, row_const=None):
    M, D = x.shape
    S = bg.shape[1]
    hp, hq = 8, 16
    assert (scw.shape[0] - 1) // 2 <= hp and (ccw.shape[0] - 1) // 2 < hq
    tm = _divisor(T, 256)
    nt = T // tm
    rp, rq = tm // hp, tm // hq
    tile = lambda b, t: b * nt + t
    cur = pl.BlockSpec((tm, S), lambda b, t: (tile(b, t), 0))
    prev = lambda h, r: pl.BlockSpec((h, S), lambda b, t: (jnp.maximum(tile(b, t) * r - 1, 0), 0))
    nxt = lambda h, r: pl.BlockSpec((h, S), lambda b, t: (jnp.minimum((tile(b, t) + 1) * r, M // h - 1), 0))
    full = lambda a: pl.BlockSpec(a.shape, lambda b, t: (0,) * a.ndim)
    row = (lambda b: row_const) if row_const is not None else (lambda b: b)
    return pl.pallas_call(
        functools.partial(_even_out_kernel, tm=tm),
        grid=(B, nt),
        in_specs=[cur, prev(hp, rp), cur, nxt(hp, rp), prev(hq, rq), cur, nxt(hq, rq),
                  full(scw), full(scb), full(ccw), full(ccb), full(lng), full(lnb), full(w),
                  pl.BlockSpec((tm, D), lambda b, t: (tile(b, t), 0)),
                  pl.BlockSpec((1, 1, D), lambda b, t: (row(b), 0, 5))],
        out_specs=pl.BlockSpec((tm, D), lambda b, t: (tile(b, t), 0)),
        out_shape=jax.ShapeDtypeStruct((M, D), f32),
        scratch_shapes=[pltpu.VMEM((tm, S), bf16), pltpu.VMEM((tm, S), f32)],
        compiler_params=_params("arbitrary", "arbitrary"),
        name="even_out",
    )(bg, p, p, p, q, q, q, scw, scb, ccw, ccb, lng, lnb, w, x, mods)


def _outproj_odd_kernel(o_ref, hf_ref, hb_ref, gr_ref, wa_ref, wb_ref, x_ref, gt_ref, out_ref):
    r = ((hf_ref[...] + hb_ref[...]) * _gelu_tanh(gr_ref[...])).astype(bf16)
    y = _dot(o_ref[...], wa_ref[...]) + _dot(r, wb_ref[...])
    out_ref[...] = x_ref[...] + gt_ref[0] * y


def _outproj_odd(o, hf, hb, z, gr_block, w, x, mods, T):
    M, D = x.shape
    Ka, Kb = o.shape[1], hf.shape[1]
    tm = _divisor(T, 256)
    nt = T // tm
    return pl.pallas_call(
        _outproj_odd_kernel,
        grid=(M // tm,),
        in_specs=[pl.BlockSpec((tm, Ka), lambda i: (i, 0)),
                  pl.BlockSpec((tm, Kb), lambda i: (i, 0)),
                  pl.BlockSpec((tm, Kb), lambda i: (i, 0)),
                  pl.BlockSpec((tm, Kb), lambda i: (i, gr_block)),
                  pl.BlockSpec((Ka, D), lambda i: (0, 0)),
                  pl.BlockSpec((Kb, D), lambda i: (Ka // Kb, 0)),
                  pl.BlockSpec((tm, D), lambda i: (i, 0)),
                  pl.BlockSpec((1, 1, D), lambda i: (i // nt, 0, 5))],
        out_specs=pl.BlockSpec((tm, D), lambda i: (i, 0)),
        out_shape=jax.ShapeDtypeStruct((M, D), f32),
        compiler_params=_params("arbitrary"),
        name="outproj_odd",
    )(o, hf, hb, z, w, w, x, mods)


def _attn_kernel(q_ref, k_ref, v_ref, kc_ref, vc_ref, qg_ref, kg_ref, bias_ref, o_ref,
                 qn_ref, kn_ref, vn_ref, kcn_ref, vcn_ref, *, rows, kr, hd, scale):
    rows_per_trip = _divisor(rows, 8)
    lanes = 2 * hd
    W = GRID_W
    ri = jnp.where(lax.broadcasted_iota(jnp.int32, (lanes, lanes), 0) < hd, 1.0, -1.0)
    ci = jnp.where(lax.broadcasted_iota(jnp.int32, (lanes, lanes), 1) < hd, 1.0, -1.0)
    seg_mean = jnp.where(ri * ci > 0.0, 1.0 / hd, 0.0).astype(bf16)

    def head_norm(x, g):
        sq = x * x
        hi = sq.astype(bf16)
        lo = (sq - hi.astype(f32)).astype(bf16)
        ms = _dot(hi, seg_mean) + _dot(lo, seg_mean)
        return x * lax.rsqrt(ms + EPS) * g

    head0 = lax.broadcasted_iota(jnp.int32, (1, lanes), 1) < hd
    qn = head_norm(q_ref[0], qg_ref[...]) * scale
    qn_ref[0] = jnp.where(head0, qn, 0.0).astype(bf16)
    qn_ref[1] = jnp.where(head0, 0.0, qn).astype(bf16)
    kn_ref[...] = head_norm(k_ref[0], kg_ref[...]).astype(bf16)
    kcn_ref[...] = head_norm(kc_ref[0], kg_ref[...]).astype(bf16)
    vn_ref[...] = v_ref[0].astype(bf16)
    vcn_ref[...] = vc_ref[0].astype(bf16)
    nt_dims = (((1,), (1,)), ((), ()))

    def body(it, carry):
        plan = []
        for sub in range(rows_per_trip):
            r = it * rows_per_trip + sub
            r0 = jnp.clip(r - kr // 2, 0, rows - kr)
            plan.append((r - r0, pl.multiple_of(r * W, W), pl.multiple_of(r0 * W, W)))
        scores = []
        for oi, qs, ks in plan:
            kw = kn_ref[pl.ds(ks, kr * W), :]
            for hh in range(2):
                qm = qn_ref[hh, pl.ds(qs, W), :]
                sl = lax.dot_general(qm, kw, nt_dims, preferred_element_type=f32) + bias_ref[0, hh, oi]
                sc = lax.dot_general(qm, kcn_ref[...], nt_dims, preferred_element_type=f32)
                scores.append((sl, sc))
        probs = []
        for sl, sc in scores:
            m = jnp.maximum(jnp.max(sl, axis=-1, keepdims=True), jnp.max(sc, axis=-1, keepdims=True))
            el = jnp.exp(sl - m)
            ec = jnp.exp(sc - m)
            den = jnp.sum(el, axis=-1, keepdims=True) + jnp.sum(ec, axis=-1, keepdims=True)
            probs.append((el.astype(bf16), ec.astype(bf16), den))
        for n, (oi, qs, ks) in enumerate(plan):
            vw = vn_ref[pl.ds(ks, kr * W), :]
            outs = []
            for hh in range(2):
                el, ec, den = probs[2 * n + hh]
                outs.append((_dot(el, vw) + _dot(ec, vcn_ref[...])) / den)
            o_ref[0, pl.ds(qs, W), :] = jnp.where(head0, outs[0], outs[1]).astype(o_ref.dtype)
        return carry

    lax.fori_loop(0, rows // rows_per_trip, body, 0)


def _bias_table(rpb, rows):
    H = rpb.shape[0]
    R = (rpb.shape[1] + 1) // 2
    Wc = (rpb.shape[2] + 1) // 2
    kr = min(R, rows)
    col = jnp.arange(GRID_W)
    col_start = jnp.clip(col - Wc // 2, 0, GRID_W - Wc)
    col_in = (col[None, :] >= col_start[:, None]) & (col[None, :] < col_start[:, None] + Wc)
    dcol = jnp.clip(col[None, :] - col[:, None] + Wc - 1, 0, 2 * Wc - 2)
    bias_cols = jnp.where(col_in, rpb.astype(f32)[:, :, dcol], NEG_INF)
    drow = jnp.arange(kr)[None, :] - jnp.arange(kr)[:, None] + R - 1
    tab = bias_cols[:, drow]
    return jnp.moveaxis(tab, 2, 3).reshape(H, kr, GRID_W, kr * GRID_W), kr


def _attention(z, zc, qg, kg, rpb, B, L, Cn, n_heads, hd):
    rows = L // GRID_W
    bias, kr = _bias_table(rpb, rows)
    npair = n_heads // 2
    lanes = 2 * hd
    bias = bias.reshape(npair, 2, kr, GRID_W, kr * GRID_W)
    qg2 = jnp.tile(qg, 2).reshape(1, lanes)
    kg2 = jnp.tile(kg, 2).reshape(1, lanes)
    seg = lambda s: pl.BlockSpec((1, L, lanes), lambda p, b: (b, 0, s * npair + p))
    segc = lambda s: pl.BlockSpec((1, Cn, lanes), lambda p, b: (b, 0, s * npair + p))
    return pl.pallas_call(
        functools.partial(_attn_kernel, rows=rows, kr=kr, hd=hd, scale=hd ** -0.5),
        grid=(npair, B),
        in_specs=[seg(0), seg(1), seg(2), segc(0), segc(1),
                  pl.BlockSpec((1, lanes), lambda p, b: (0, 0)),
                  pl.BlockSpec((1, lanes), lambda p, b: (0, 0)),
                  pl.BlockSpec((1, 2, kr, GRID_W, kr * GRID_W), lambda p, b: (p, 0, 0, 0, 0))],
        out_specs=pl.BlockSpec((1, L, lanes), lambda p, b: (b, 0, p)),
        out_shape=jax.ShapeDtypeStruct((B, L, npair * lanes), bf16),
        scratch_shapes=[pltpu.VMEM((2, L, lanes), bf16), pltpu.VMEM((L, lanes), bf16),
                        pltpu.VMEM((L, lanes), bf16), pltpu.VMEM((Cn, lanes), bf16),
                        pltpu.VMEM((Cn, lanes), bf16)],
        compiler_params=_params("arbitrary", "arbitrary"),
        name="attention",
    )(z, z, z, zc, zc, qg2, kg2, bias)


def _lru_kernel(xfp_ref, xf_ref, xfn_ref, xbp_ref, xb_ref, xbn_ref, h0f_ref, h0b_ref, cw_ref, cb_ref,
                wbd_ref, gb_ref, lam_ref, hf_ref, hb_ref, sf_ref, sb_ref,
                xw_ref, xc_ref, a_ref, u_ref, st_ref, *, tt, tc, nj, chunk):
    B, C1 = xf_ref.shape[1], xf_ref.shape[2]
    kw = cw_ref.shape[0]
    lo = (kw - 1) // 2
    hi = kw - 1 - lo
    j = pl.program_id(0)

    @pl.when(j == 0)
    def _():
        st_ref[0] = h0f_ref[...]
        st_ref[1] = h0b_ref[...]

    for d, (xp_ref, xm_ref, xn_ref, jj) in enumerate(((xfp_ref, xf_ref, xfn_ref, j),
                                                     (xbp_ref, xb_ref, xbn_ref, nj - 1 - j))):
        xw_ref[d, 0:lo] = jnp.where(jj == 0, 0.0, xp_ref[...])
        xw_ref[d, lo:lo + tt] = xm_ref[...]
        xw_ref[d, lo + tt:lo + tt + hi] = jnp.where(jj == nj - 1, 0.0, xn_ref[...])

        def conv_body(ci, carry, d=d):
            t0 = ci * tc
            acc = cw_ref[0][None] * xw_ref[d, pl.ds(t0, tc)]
            for k in range(1, kw):
                acc = acc + cw_ref[k][None] * xw_ref[d, pl.ds(t0 + k, tc)]
            xc_ref[d, pl.ds(t0, tc)] = acc + cb_ref[...][None]
            return carry

        lax.fori_loop(0, tt // tc, conv_body, 0)

        lam = lam_ref[d]
        neg = -lam
        softplus = jnp.maximum(neg, 0.0) + jnp.log(1.0 + jnp.exp(-jnp.abs(neg)))
        rate = -LRU_C * softplus
        for c in range(C1 // chunk):
            cs = slice(c * chunk, (c + 1) * chunk)
            xs = xc_ref[d, :, :, cs].reshape(tt * B, chunk)
            xh = xs.astype(bf16)
            r = jax.nn.sigmoid(_dot(xh, wbd_ref[d, 0, c]) + gb_ref[d, 0][:, cs])
            i = jax.nn.sigmoid(_dot(xh, wbd_ref[d, 1, c]) + gb_ref[d, 1][:, cs])
            log_a = rate[:, cs] * r
            a = jnp.exp(log_a)
            mult = jnp.sqrt(jnp.tanh(-log_a) * (a * a + 1.0))
            a_ref[d, :, :, cs] = a.reshape(tt, B, chunk)
            u_ref[d, :, :, cs] = (mult * i * xs).reshape(tt, B, chunk)

    def scan_body(s, carry):
        hf, hb = carry
        tb = tt - 1 - s
        hf = a_ref[0, s] * hf + u_ref[0, s]
        hb = a_ref[1, tb] * hb + u_ref[1, tb]
        hf_ref[s] = hf
        hb_ref[tb] = hb
        return hf, hb

    hf, hb = lax.fori_loop(0, tt, scan_body, (st_ref[0], st_ref[1]))
    st_ref[0] = hf
    st_ref[1] = hb
    sf_ref[...] = hf
    sb_ref[...] = hb


def _lru(x, h0f, h0b, cw, cb, wbd, gb, lam):
    T, B, C1 = x.shape
    kw = cw.shape[0]
    lo = (kw - 1) // 2
    hi = kw - 1 - lo
    assert lo == 1 and hi == 2
    chunk = wbd.shape[-1]
    tt = _divisor(T, 64)
    assert tt % hi == 0
    nj = T // tt
    full = lambda a: pl.BlockSpec(a.shape, lambda j: (0,) * a.ndim)
    state = jax.ShapeDtypeStruct((B, C1), f32)
    seq = jax.ShapeDtypeStruct((T, B, C1), f32)

    def windows(chunk_of):
        return [pl.BlockSpec((lo, B, C1), lambda j: (jnp.maximum(chunk_of(j) * tt - 1, 0), 0, 0)),
                pl.BlockSpec((tt, B, C1), lambda j: (chunk_of(j), 0, 0)),
                pl.BlockSpec((hi, B, C1),
                             lambda j: (jnp.minimum((chunk_of(j) + 1) * (tt // hi), T // hi - 1), 0, 0))]

    fwd = lambda j: j
    bwd = lambda j: nj - 1 - j
    return pl.pallas_call(
        functools.partial(_lru_kernel, tt=tt, tc=8, nj=nj, chunk=chunk),
        grid=(nj,),
        in_specs=windows(fwd) + windows(bwd) + [full(h0f), full(h0b), full(cw), full(cb), full(wbd),
                                                full(gb), full(lam)],
        out_specs=[pl.BlockSpec((tt, B, C1), lambda j: (j, 0, 0)),
                   pl.BlockSpec((tt, B, C1), lambda j: (nj - 1 - j, 0, 0)),
                   pl.BlockSpec((B, C1), lambda j: (0, 0)),
                   pl.BlockSpec((B, C1), lambda j: (0, 0))],
        out_shape=[seq, seq, state, state],
        scratch_shapes=[pltpu.VMEM((2, tt + kw - 1, B, C1), f32), pltpu.VMEM((2, tt, B, C1), f32),
                        pltpu.VMEM((2, tt, B, C1), f32), pltpu.VMEM((2, tt, B, C1), f32),
                        pltpu.VMEM((2, B, C1), f32)],
        compiler_params=_params("arbitrary"),
        name="rglru",
    )(x, x, x, x, x, x, h0f, h0b, cw, cb, wbd, gb, lam)


def _block_diag_chunks(w, chunk):
    *lead, nb, bs, _ = w.shape
    per = chunk // bs
    w6 = w.reshape(*lead, nb // per, per, bs, bs)
    eye = jnp.eye(per, dtype=w.dtype)
    dense = jnp.einsum('...ckij,kl->...ckilj', w6, eye)
    return dense.reshape(*lead, nb // per, chunk, chunk)


def _to_tm(a, B, T):
    return jnp.swapaxes(a.reshape(B, T, a.shape[-1]), 0, 1)


def _to_bm(a):
    T, B, C = a.shape
    return jnp.swapaxes(a, 0, 1).reshape(B * T, C)


def _even_mixer(xs, g, mods, Ts, rows_const, w_in, sc_w, sc_b, cc_w, cc_b, ln_g, ln_b, w_out, B):
    D = w_in.shape[0]
    S = sc_w.shape[1]
    ts = _divisor(S, MXU_DIM)
    w_r = w_in.reshape(D, 5, S // ts, ts).transpose(0, 2, 1, 3).reshape(D, 5 * S).astype(bf16)
    w_o = w_out.astype(bf16)
    row1 = lambda a: a.reshape(1, S)
    outs = []
    for x, T, rc in zip(xs, Ts, rows_const):
        bg, p, q = _evenin(x, g, mods, w_r, ts, T, rc)
        outs.append(_even_out(bg, p, q, sc_w, row1(sc_b), cc_w, row1(cc_b), row1(ln_g), row1(ln_b), w_o, x, mods,
                              B, T, rc))
    return outs


def _odd_mixer_lat(xl, xc, g, mods, B, L, Cn, w_in, q_g, k_g, rpb, conv_w, conv_b, gate_w, gate_b, lam, w_out):
    D = w_in.shape[0]
    hd = q_g.shape[0]
    n_heads = rpb.shape[0]
    Dn = n_heads * hd
    C1 = conv_w.shape[1]
    w = w_in.astype(bf16)
    z = _modmm(xl, g, mods, w, L)
    zc = _modmm(xc, g, mods, w[:, Dn:3 * Dn + C1], Cn, row_const=B)
    o = _attention(z.reshape(B, L, -1), zc.reshape(B, Cn, -1), q_g, k_g, rpb, B, L, Cn, n_heads, hd)
    chunk = min(MXU_DIM, C1)
    wbd = _block_diag_chunks(gate_w, chunk).astype(bf16)
    bcast = lambda a: jnp.broadcast_to(a[..., None, :], a.shape[:-1] + (B, a.shape[-1]))
    cw, cb = bcast(conv_w), bcast(conv_b)
    gb = gate_b[:, :, None, :]
    lam3 = lam[:, None, :]
    zero = jnp.zeros((B, C1), f32)
    xrc = _to_tm(zc[:, 2 * Dn:2 * Dn + C1], B, Cn)
    _, _, sf, sb = _lru(xrc, zero, zero, cw, cb, wbd, gb, lam3)
    xr = _to_tm(z[:, 3 * Dn:3 * Dn + C1], B, L)
    hf, hb, _, _ = _lru(xr, sf, sb, cw, cb, wbd, gb, lam3)
    assert Dn == C1
    return _outproj_odd(o.reshape(B * L, Dn), _to_bm(hf), _to_bm(hb), z, (3 * Dn + C1) // C1,
                        w_out.astype(bf16), xl, mods, L)


def kernel(x, c, ctx, c_ctx, w_mod, b_mod, norm_g, ffn_w_in, ffn_w_out, ev_w_in, sc_w, sc_b, cc_w, cc_b, cc_ln_g, cc_ln_b, ev_w_out, od_w_in, q_norm_g, k_norm_g, na_rpb, lru_conv_w, lru_conv_b, lru_gate_w, lru_gate_b, lru_lam, od_w_out):
    B, L, D = x.shape
    Cn = ctx.shape[1]
    depth = w_mod.shape[0]
    R = -(-(B + 1) // 8) * 8
    cvec = jnp.zeros((R, D), f32).at[:B].set(c).at[B].set(c_ctx)
    mods_all = _mod_table(cvec, w_mod, b_mod)
    xl = x.reshape(B * L, D)
    xc = ctx.reshape(B * Cn, D)
    for l in range(depth):
        last = l == depth - 1
        odd = l % 2 == 1
        j = l // 2
        ctx_in = odd or not last
        ctx_out = not last
        mods = mods_all[l].reshape(R, 1, N_MOD * D)
        g = norm_g[l].reshape(3, 1, D)
        wi = ffn_w_in[l].astype(bf16)
        wo = ffn_w_out[l].astype(bf16)
        xl = _ffn(xl, g[0], mods, 0, wi[0], wo[0], L)
        if ctx_in:
            xc = _ffn(xc, g[0], mods, 0, wi[0], wo[0], Cn, row_const=B)
        if odd:
            assert ctx_in and not ctx_out
            xl = _odd_mixer_lat(xl, xc, g[1], mods, B, L, Cn, od_w_in[j], q_norm_g[j], k_norm_g[j], na_rpb[j],
                                lru_conv_w[j], lru_conv_b[j], lru_gate_w[j], lru_gate_b[j], lru_lam[j],
                                od_w_out[j])
        else:
            streams = [(xl, L, None)] + ([(xc, Cn, B)] if ctx_out else [])
            outs = _even_mixer([s[0] for s in streams], g[1], mods, [s[1] for s in streams],
                               [s[2] for s in streams], ev_w_in[j], sc_w[j], sc_b[j], cc_w[j], cc_b[j],
                               cc_ln_g[j], cc_ln_b[j], ev_w_out[j], B)
            xl = outs[0]
            if ctx_out:
                xc = outs[1]
        xl = _ffn(xl, g[2], mods, 6, wi[1], wo[1], L)
        if ctx_out:
            xc = _ffn(xc, g[2], mods, 6, wi[1], wo[1], Cn, row_const=B)
    return xl.reshape(B, L, D)
```

```python
import functools
import math

import jax
import jax.numpy as jnp
from jax import lax
from jax.experimental import pallas as pl
from jax.experimental.pallas import tpu as pltpu

f32 = jnp.float32
bf16 = jnp.bfloat16

GRID_W = 64
LRU_C = 8.0
EPS = 1e-6
NEG_INF = -1e30
N_MOD = 9
MXU_DIM = 256
VMEM_LIMIT_BYTES = 60 * 1024 * 1024


def _divisor(n, pref):
    d = min(n, pref)
    while n % d:
        d -= 1
    return d


def _params(*sem):
    return pltpu.CompilerParams(dimension_semantics=sem, vmem_limit_bytes=VMEM_LIMIT_BYTES)


def _row_tiling(M, T, row_const, pref):
    if row_const is not None:
        return _divisor(M, pref), (lambda i: row_const)
    tm = _divisor(T, pref)
    nt = T // tm
    return tm, (lambda i: i // nt)


def _silu(x):
    return x * jax.nn.sigmoid(x)


def _gelu_tanh(x):
    return 0.5 * x * (1.0 + jnp.tanh(0.7978845608028654 * (x + 0.044715 * (x * x * x))))


MOD_ROWS = 16
MOD_CHUNKS = 8


def _modulate_rows(x_ref, g_ref, sh_ref, sc_ref, hm_ref, zero_ref=None):
    tm, D = x_ref.shape
    rc = _divisor(tm, MOD_ROWS)
    per_trip = _divisor(tm // rc, MOD_CHUNKS)
    gain = g_ref[...] * (1.0 + sc_ref[0])
    shift = sh_ref[0]

    def body(i, carry):
        for sub in range(per_trip):
            r0 = pl.multiple_of((i * per_trip + sub) * rc, rc)
            x = x_ref[pl.ds(r0, rc), :]
            ms = jnp.mean(x * x, axis=-1, keepdims=True)
            hm_ref[pl.ds(r0, rc), :] = (x * lax.rsqrt(ms + EPS) * gain + shift).astype(bf16)
            if zero_ref is not None:
                zero_ref[pl.ds(r0, rc), :] = jnp.zeros((rc, D), zero_ref.dtype)
        return carry

    lax.fori_loop(0, tm // (rc * per_trip), body, 0)


def _dot(a, b):
    return jnp.dot(a, b, preferred_element_type=f32)


def _mod_spec(D, row, k):
    return pl.BlockSpec((1, 1, D), lambda i, j: (row(i), 0, k))


def _mod_kernel(c_ref, w_ref, b_ref, o_ref):
    s = _silu(c_ref[...]).astype(bf16)
    o_ref[0] = _dot(s, w_ref[0].astype(bf16)) + b_ref[0]


def _mod_table(cvec, w_mod, b_mod):
    depth, D, N = w_mod.shape
    R = cvec.shape[0]
    tn = _divisor(N, 1024)
    return pl.pallas_call(
        _mod_kernel,
        grid=(depth, N // tn),
        in_specs=[pl.BlockSpec((R, D), lambda l, n: (0, 0)),
                  pl.BlockSpec((1, D, tn), lambda l, n: (l, 0, n)),
                  pl.BlockSpec((1, 1, tn), lambda l, n: (l, 0, n))],
        out_specs=pl.BlockSpec((1, R, tn), lambda l, n: (l, 0, n)),
        out_shape=jax.ShapeDtypeStruct((depth, R, N), f32),
        compiler_params=_params("arbitrary", "arbitrary"),
        name="mod_table",
    )(cvec, w_mod, b_mod.reshape(depth, 1, N))


def _ffn_kernel(x_ref, g_ref, sh_ref, sc_ref, gt_ref, wg_ref, wu_ref, wo_ref, o_ref, hm_ref, *, nf):
    f = pl.program_id(1)

    @pl.when(f == 0)
    def _():
        _modulate_rows(x_ref, g_ref, sh_ref, sc_ref, hm_ref, zero_ref=o_ref)

    h = hm_ref[...]
    gate = _dot(h, wg_ref[...])
    up = _dot(h, wu_ref[...])
    o_ref[...] += _dot((_silu(gate) * up).astype(bf16), wo_ref[...])

    @pl.when(f == nf - 1)
    def _():
        o_ref[...] = x_ref[...] + 0.5 * gt_ref[0] * o_ref[...]


FFN_HIDDEN_TILE = 512


def _ffn(x, g, mods, k0, w_in, w_out, which, T, row_const=None):
    M, D = x.shape
    l, a = which
    F = w_out.shape[2]
    tf = _divisor(F, FFN_HIDDEN_TILE)
    nf = F // tf
    tm, row = _row_tiling(M, T, row_const, 1024)
    return pl.pallas_call(
        functools.partial(_ffn_kernel, nf=nf),
        grid=(M // tm, nf),
        in_specs=[pl.BlockSpec((tm, D), lambda i, f: (i, 0)),
                  pl.BlockSpec((1, D), lambda i, f: (0, 0)),
                  _mod_spec(D, row, k0), _mod_spec(D, row, k0 + 1), _mod_spec(D, row, k0 + 2),
                  pl.BlockSpec((None, None, D, tf), lambda i, f: (l, a, 0, f)),
                  pl.BlockSpec((None, None, D, tf), lambda i, f: (l, a, 0, nf + f)),
                  pl.BlockSpec((None, None, tf, D), lambda i, f: (l, a, f, 0))],
        out_specs=pl.BlockSpec((tm, D), lambda i, f: (i, 0)),
        out_shape=jax.ShapeDtypeStruct((M, D), f32),
        scratch_shapes=[pltpu.VMEM((tm, D), bf16)],
        compiler_params=_params("arbitrary", "arbitrary"),
        name="ffn",
    )(x, g, mods, mods, mods, w_in, w_in, w_out)


def _modulate_once(x_ref, g_ref, sh_ref, sc_ref, hm_ref):
    @pl.when(pl.program_id(1) == 0)
    def _():
        _modulate_rows(x_ref, g_ref, sh_ref, sc_ref, hm_ref)


def _modmm_kernel(x_ref, g_ref, sh_ref, sc_ref, w_ref, o_ref, hm_ref):
    _modulate_once(x_ref, g_ref, sh_ref, sc_ref, hm_ref)
    o_ref[...] = _dot(hm_ref[...], w_ref[...])


def _inproj_specs(D, tm, wn, row, n0=0):
    return [pl.BlockSpec((tm, D), lambda i, n: (i, 0)),
            pl.BlockSpec((1, D), lambda i, n: (0, 0)),
            _mod_spec(D, row, 3), _mod_spec(D, row, 4),
            pl.BlockSpec((D, wn), lambda i, n: (0, n0 + n))]


def _modmm(x, g, mods, w, T, row_const=None, cols=None):
    M, D = x.shape
    c0, c1 = cols if cols is not None else (0, w.shape[1])
    N = c1 - c0
    tm, row = _row_tiling(M, T, row_const, 1024)
    tn = _divisor(math.gcd(N, c0), 1024)
    return pl.pallas_call(
        _modmm_kernel,
        grid=(M // tm, N // tn),
        in_specs=_inproj_specs(D, tm, tn, row, c0 // tn),
        out_specs=pl.BlockSpec((tm, tn), lambda i, n: (i, n)),
        out_shape=jax.ShapeDtypeStruct((M, N), f32),
        scratch_shapes=[pltpu.VMEM((tm, D), bf16)],
        compiler_params=_params("arbitrary", "arbitrary"),
        name="modmm",
    )(x, g, mods, mods, w)


def _evenin_kernel(x_ref, g_ref, sh_ref, sc_ref, w_ref, bg_ref, p_ref, q_ref, hm_ref, *, ts):
    _modulate_once(x_ref, g_ref, sh_ref, sc_ref, hm_ref)
    z = _dot(hm_ref[...], w_ref[...])
    bg_ref[...] = z[:, 0:ts]
    p_ref[...] = z[:, ts:2 * ts] * z[:, 2 * ts:3 * ts]
    q_ref[...] = z[:, 3 * ts:4 * ts] * jax.nn.sigmoid(z[:, 4 * ts:5 * ts])


def _evenin(x, g, mods, w_r, ts, T, row_const=None):
    M, D = x.shape
    S = w_r.shape[1] // 5
    tm, row = _row_tiling(M, T, row_const, 1024)
    out = jax.ShapeDtypeStruct((M, S), f32)
    ospec = pl.BlockSpec((tm, ts), lambda i, n: (i, n))
    return pl.pallas_call(
        functools.partial(_evenin_kernel, ts=ts),
        grid=(M // tm, S // ts),
        in_specs=_inproj_specs(D, tm, 5 * ts, row),
        out_specs=[ospec, ospec, ospec],
        out_shape=[out, out, out],
        scratch_shapes=[pltpu.VMEM((tm, D), bf16)],
        compiler_params=_params("arbitrary", "arbitrary"),
        name="evenin",
    )(x, g, mods, mods, w_r)


def _shifted_taps(win, w_ref, cs, lo, halo, tm):
    acc = None
    for s in range(8):
        taps = [k for k in range(w_ref.shape[0]) if (halo + k - lo + s) % 8 == 0]
        if not taps:
            continue
        wr = win if s == 0 else pltpu.roll(win, s, axis=0)
        for k in taps:
            st = halo + k - lo + s
            term = w_ref[k:k + 1, cs] * wr[st:st + tm]
            acc = term if acc is None else acc + term
    return acc


def _even_out_kernel(bg_ref, pp_ref, p_ref, pn_ref, qp_ref, q_ref, qn_ref, scw_ref, scb_ref, ccw_ref, ccb_ref,
                     lng_ref, lnb_ref, w_ref, x_ref, gt_ref, o_ref, ysc_ref, u_ref, *, tm):
    S = bg_ref.shape[1]
    hp, hq = pp_ref.shape[0], qp_ref.shape[0]
    t = pl.program_id(1)
    first = t == 0
    last = t == pl.num_programs(1) - 1
    lo_sc = (scw_ref.shape[0] - 1) // 2
    lo_cc = (ccw_ref.shape[0] - 1) // 2
    for c in range(S // 128):
        cs = slice(c * 128, (c + 1) * 128)
        pw = jnp.concatenate([jnp.where(first, 0.0, pp_ref[:, cs]), p_ref[:, cs],
                              jnp.where(last, 0.0, pn_ref[:, cs])], axis=0)
        conv = _shifted_taps(pw, scw_ref, cs, lo_sc, hp, tm)
        ysc_ref[:, cs] = (bg_ref[:, cs] * (conv + scb_ref[:, cs])).astype(bf16)
    y = _dot(ysc_ref[...], w_ref[0:S, :])
    for c in range(S // 128):
        cs = slice(c * 128, (c + 1) * 128)
        qw = jnp.concatenate([jnp.where(first, 0.0, qp_ref[:, cs]), q_ref[:, cs],
                              jnp.where(last, 0.0, qn_ref[:, cs])], axis=0)
        u_ref[:, cs] = _shifted_taps(qw, ccw_ref, cs, lo_cc, hq, tm) + ccb_ref[:, cs]
    u = u_ref[...]
    mu = jnp.mean(u, axis=-1, keepdims=True)
    xc = u - mu
    var = jnp.mean(xc * xc, axis=-1, keepdims=True)
    ycc = _silu(xc * lax.rsqrt(var + EPS) * lng_ref[...] + lnb_ref[...]).astype(bf16)
    y = y + _dot(ycc, w_ref[S:2 * S, :])
    o_ref[...] = x_ref[...] + gt_ref[0] * y


def _even_out(bg, p, q, scw, scb, ccw, ccb, lng, lnb, w, x, mods, B, T, row_const=None):
    M, D = x.shape
    S = bg.shape[1]
    hp, hq = 8, 16
    assert (scw.shape[0] - 1) // 2 <= hp and (ccw.shape[0] - 1) // 2 < hq
    tm = _divisor(T, 256)
    nt = T // tm
    rp, rq = tm // hp, tm // hq
    tile = lambda b, t: b * nt + t
    cur = pl.BlockSpec((tm, S), lambda b, t: (tile(b, t), 0))
    prev = lambda h, r: pl.BlockSpec((h, S), lambda b, t: (jnp.maximum(tile(b, t) * r - 1, 0), 0))
    nxt = lambda h, r: pl.BlockSpec((h, S), lambda b, t: (jnp.minimum((tile(b, t) + 1) * r, M // h - 1), 0))
    full = lambda a: pl.BlockSpec(a.shape, lambda b, t: (0,) * a.ndim)
    row = (lambda b: row_const) if row_const is not None else (lambda b: b)
    return pl.pallas_call(
        functools.partial(_even_out_kernel, tm=tm),
        grid=(B, nt),
        in_specs=[cur, prev(hp, rp), cur, nxt(hp, rp), prev(hq, rq), cur, nxt(hq, rq),
                  full(scw), full(scb), full(ccw), full(ccb), full(lng), full(lnb), full(w),
                  pl.BlockSpec((tm, D), lambda b, t: (tile(b, t), 0)),
                  pl.BlockSpec((1, 1, D), lambda b, t: (row(b), 0, 5))],
        out_specs=pl.BlockSpec((tm, D), lambda b, t: (tile(b, t), 0)),
        out_shape=jax.ShapeDtypeStruct((M, D), f32),
        scratch_shapes=[pltpu.VMEM((tm, S), bf16), pltpu.VMEM((tm, S), f32)],
        compiler_params=_params("arbitrary", "arbitrary"),
        name="even_out",
    )(bg, p, p, p, q, q, q, scw, scb, ccw, ccb, lng, lnb, w, x, mods)


def _outproj_odd_kernel(o_ref, hf_ref, hb_ref, gr_ref, wa_ref, wb_ref, x_ref, gt_ref, out_ref):
    r = ((hf_ref[...] + hb_ref[...]) * _gelu_tanh(gr_ref[...])).astype(bf16)
    y = _dot(o_ref[...], wa_ref[...]) + _dot(r, wb_ref[...])
    out_ref[...] = x_ref[...] + gt_ref[0] * y


def _outproj_odd(o, hf, hb, z, gr_block, w, x, mods, T):
    M, D = x.shape
    Ka, Kb = o.shape[1], hf.shape[1]
    tm = _divisor(T, 256)
    nt = T // tm
    return pl.pallas_call(
        _outproj_odd_kernel,
        grid=(M // tm,),
        in_specs=[pl.BlockSpec((tm, Ka), lambda i: (i, 0)),
                  pl.BlockSpec((tm, Kb), lambda i: (i, 0)),
                  pl.BlockSpec((tm, Kb), lambda i: (i, 0)),
                  pl.BlockSpec((tm, Kb), lambda i: (i, gr_block)),
                  pl.BlockSpec((Ka, D), lambda i: (0, 0)),
                  pl.BlockSpec((Kb, D), lambda i: (Ka // Kb, 0)),
                  pl.BlockSpec((tm, D), lambda i: (i, 0)),
                  pl.BlockSpec((1, 1, D), lambda i: (i // nt, 0, 5))],
        out_specs=pl.BlockSpec((tm, D), lambda i: (i, 0)),
        out_shape=jax.ShapeDtypeStruct((M, D), f32),
        compiler_params=_params("arbitrary"),
        name="outproj_odd",
    )(o, hf, hb, z, w, w, x, mods)


def _attn_kernel(q_ref, k_ref, v_ref, kc_ref, vc_ref, qg_ref, kg_ref, bias_ref, o_ref,
                 qn_ref, kn_ref, vn_ref, kcn_ref, vcn_ref, *, rows, kr, win_rows, hd, scale):
    rows_per_trip = _divisor(rows, 8)
    lanes = 2 * hd
    W = GRID_W
    ri = jnp.where(lax.broadcasted_iota(jnp.int32, (lanes, lanes), 0) < hd, 1.0, -1.0)
    ci = jnp.where(lax.broadcasted_iota(jnp.int32, (lanes, lanes), 1) < hd, 1.0, -1.0)
    seg_mean = jnp.where(ri * ci > 0.0, 1.0 / hd, 0.0).astype(bf16)

    def head_norm(x, g):
        sq = x * x
        hi = sq.astype(bf16)
        lo = (sq - hi.astype(f32)).astype(bf16)
        ms = _dot(hi, seg_mean) + _dot(lo, seg_mean)
        return x * lax.rsqrt(ms + EPS) * g

    head0 = lax.broadcasted_iota(jnp.int32, (1, lanes), 1) < hd
    qn = head_norm(q_ref[0], qg_ref[...]) * scale
    qn_ref[0] = jnp.where(head0, qn, 0.0).astype(bf16)
    qn_ref[1] = jnp.where(head0, 0.0, qn).astype(bf16)
    kn_ref[...] = head_norm(k_ref[0], kg_ref[...]).astype(bf16)
    kcn_ref[...] = head_norm(kc_ref[0], kg_ref[...]).astype(bf16)
    vn_ref[...] = v_ref[0].astype(bf16)
    vcn_ref[...] = vc_ref[0].astype(bf16)
    nt_dims = (((1,), (1,)), ((), ()))

    def body(it, carry):
        plan = []
        for sub in range(rows_per_trip):
            r = it * rows_per_trip + sub
            r0 = jnp.clip(r - kr // 2, 0, rows - kr)
            plan.append((r - r0, pl.multiple_of(r * W, W), pl.multiple_of(r0 * W, W)))
        scores = []
        for oi, qs, ks in plan:
            kw = kn_ref[pl.ds(ks, kr * W), :]
            for hh in range(2):
                qm = qn_ref[hh, pl.ds(qs, W), :]
                bias = jnp.concatenate([bias_ref[0, hh, 2 * jj - oi + win_rows - 1] for jj in range(kr // 2)],
                                       axis=1)
                sl = lax.dot_general(qm, kw, nt_dims, preferred_element_type=f32) + bias
                sc = lax.dot_general(qm, kcn_ref[...], nt_dims, preferred_element_type=f32)
                scores.append((sl, sc))
        probs = []
        for sl, sc in scores:
            m = jnp.maximum(jnp.max(sl, axis=-1, keepdims=True), jnp.max(sc, axis=-1, keepdims=True))
            el = jnp.exp(sl - m)
            ec = jnp.exp(sc - m)
            den = jnp.sum(el, axis=-1, keepdims=True) + jnp.sum(ec, axis=-1, keepdims=True)
            probs.append((el.astype(bf16), ec.astype(bf16), den))
        for n, (oi, qs, ks) in enumerate(plan):
            vw = vn_ref[pl.ds(ks, kr * W), :]
            outs = []
            for hh in range(2):
                el, ec, den = probs[2 * n + hh]
                outs.append((_dot(el, vw) + _dot(ec, vcn_ref[...])) / den)
            o_ref[0, pl.ds(qs, W), :] = jnp.where(head0, outs[0], outs[1]).astype(o_ref.dtype)
        return carry

    lax.fori_loop(0, rows // rows_per_trip, body, 0)


def _bias_table(rpb):
    Wc = (rpb.shape[2] + 1) // 2
    col = jnp.arange(GRID_W)
    col_start = jnp.clip(col - Wc // 2, 0, GRID_W - Wc)
    col_in = (col[None, :] >= col_start[:, None]) & (col[None, :] < col_start[:, None] + Wc)
    dcol = jnp.clip(col[None, :] - col[:, None] + Wc - 1, 0, 2 * Wc - 2)
    bias_cols = jnp.where(col_in, rpb.astype(f32)[:, :, dcol], NEG_INF)
    return jnp.concatenate([bias_cols[:, :-1], bias_cols[:, 1:]], axis=-1)


def _attention(z, zc, qg, kg, rpb, B, L, Cn, n_heads, hd):
    rows = L // GRID_W
    win_rows = (rpb.shape[1] + 1) // 2
    kr = min(win_rows, rows)
    assert kr % 2 == 0
    npair = n_heads // 2
    lanes = 2 * hd
    nd = 2 * win_rows - 2
    bias = _bias_table(rpb).reshape(npair, 2, nd, GRID_W, 2 * GRID_W)
    qg2 = jnp.tile(qg, 2).reshape(1, lanes)
    kg2 = jnp.tile(kg, 2).reshape(1, lanes)
    seg = lambda s: pl.BlockSpec((1, L, lanes), lambda p, b: (b, 0, s * npair + p))
    segc = lambda s: pl.BlockSpec((1, Cn, lanes), lambda p, b: (b, 0, s * npair + p))
    return pl.pallas_call(
        functools.partial(_attn_kernel, rows=rows, kr=kr, win_rows=win_rows, hd=hd, scale=hd ** -0.5),
        grid=(npair, B),
        in_specs=[seg(0), seg(1), seg(2), segc(0), segc(1),
                  pl.BlockSpec((1, lanes), lambda p, b: (0, 0)),
                  pl.BlockSpec((1, lanes), lambda p, b: (0, 0)),
                  pl.BlockSpec((1, 2, nd, GRID_W, 2 * GRID_W), lambda p, b: (p, 0, 0, 0, 0))],
        out_specs=pl.BlockSpec((1, L, lanes), lambda p, b: (b, 0, p)),
        out_shape=jax.ShapeDtypeStruct((B, L, npair * lanes), bf16),
        scratch_shapes=[pltpu.VMEM((2, L, lanes), bf16), pltpu.VMEM((L, lanes), bf16),
                        pltpu.VMEM((L, lanes), bf16), pltpu.VMEM((Cn, lanes), bf16),
                        pltpu.VMEM((Cn, lanes), bf16)],
        compiler_params=_params("arbitrary", "arbitrary"),
        name="attention",
    )(z, z, z, zc, zc, qg2, kg2, bias)


def _lru_kernel(xfp_ref, xf_ref, xfn_ref, xbp_ref, xb_ref, xbn_ref, h0f_ref, h0b_ref, cw_ref, cb_ref,
                wbd_ref, gb_ref, lam_ref, hf_ref, hb_ref, sf_ref, sb_ref,
                xw_ref, xc_ref, a_ref, u_ref, st_ref, *, tt, tc, nj, chunk):
    B, C1 = xf_ref.shape[1], xf_ref.shape[2]
    kw = cw_ref.shape[0]
    lo = (kw - 1) // 2
    hi = kw - 1 - lo
    j = pl.program_id(0)

    @pl.when(j == 0)
    def _():
        st_ref[0] = h0f_ref[...]
        st_ref[1] = h0b_ref[...]

    for d, (xp_ref, xm_ref, xn_ref, jj) in enumerate(((xfp_ref, xf_ref, xfn_ref, j),
                                                     (xbp_ref, xb_ref, xbn_ref, nj - 1 - j))):
        xw_ref[d, 0:lo] = jnp.where(jj == 0, 0.0, xp_ref[...])
        xw_ref[d, lo:lo + tt] = xm_ref[...]
        xw_ref[d, lo + tt:lo + tt + hi] = jnp.where(jj == nj - 1, 0.0, xn_ref[...])

        def conv_body(ci, carry, d=d):
            t0 = ci * tc
            acc = cw_ref[0][None] * xw_ref[d, pl.ds(t0, tc)]
            for k in range(1, kw):
                acc = acc + cw_ref[k][None] * xw_ref[d, pl.ds(t0 + k, tc)]
            xc_ref[d, pl.ds(t0, tc)] = acc + cb_ref[...][None]
            return carry

        lax.fori_loop(0, tt // tc, conv_body, 0)

        lam = lam_ref[d]
        neg = -lam
        softplus = jnp.maximum(neg, 0.0) + jnp.log(1.0 + jnp.exp(-jnp.abs(neg)))
        rate = -LRU_C * softplus
        for c in range(C1 // chunk):
            cs = slice(c * chunk, (c + 1) * chunk)
            xs = xc_ref[d, :, :, cs].reshape(tt * B, chunk)
            xh = xs.astype(bf16)
            r = jax.nn.sigmoid(_dot(xh, wbd_ref[d, 0, c]) + gb_ref[d, 0][:, cs])
            i = jax.nn.sigmoid(_dot(xh, wbd_ref[d, 1, c]) + gb_ref[d, 1][:, cs])
            log_a = rate[:, cs] * r
            a = jnp.exp(log_a)
            mult = jnp.sqrt(jnp.tanh(-log_a) * (a * a + 1.0))
            a_ref[d, :, :, cs] = a.reshape(tt, B, chunk)
            u_ref[d, :, :, cs] = (mult * i * xs).reshape(tt, B, chunk)

    def scan_body(s, carry):
        hf, hb = carry
        tb = tt - 1 - s
        hf = a_ref[0, s] * hf + u_ref[0, s]
        hb = a_ref[1, tb] * hb + u_ref[1, tb]
        hf_ref[s] = hf
        hb_ref[tb] = hb
        return hf, hb

    hf, hb = lax.fori_loop(0, tt, scan_body, (st_ref[0], st_ref[1]))
    st_ref[0] = hf
    st_ref[1] = hb
    sf_ref[...] = hf
    sb_ref[...] = hb


def _lru(x, h0f, h0b, cw, cb, wbd, gb, lam):
    T, B, C1 = x.shape
    kw = cw.shape[0]
    lo = (kw - 1) // 2
    hi = kw - 1 - lo
    assert lo == 1 and hi == 2
    chunk = wbd.shape[-1]
    tt = _divisor(T, 64)
    assert tt % hi == 0
    nj = T // tt
    full = lambda a: pl.BlockSpec(a.shape, lambda j: (0,) * a.ndim)
    state = jax.ShapeDtypeStruct((B, C1), f32)
    seq = jax.ShapeDtypeStruct((T, B, C1), f32)

    def windows(chunk_of):
        return [pl.BlockSpec((lo, B, C1), lambda j: (jnp.maximum(chunk_of(j) * tt - 1, 0), 0, 0)),
                pl.BlockSpec((tt, B, C1), lambda j: (chunk_of(j), 0, 0)),
                pl.BlockSpec((hi, B, C1),
                             lambda j: (jnp.minimum((chunk_of(j) + 1) * (tt // hi), T // hi - 1), 0, 0))]

    fwd = lambda j: j
    bwd = lambda j: nj - 1 - j
    return pl.pallas_call(
        functools.partial(_lru_kernel, tt=tt, tc=8, nj=nj, chunk=chunk),
        grid=(nj,),
        in_specs=windows(fwd) + windows(bwd) + [full(h0f), full(h0b), full(cw), full(cb), full(wbd),
                                                full(gb), full(lam)],
        out_specs=[pl.BlockSpec((tt, B, C1), lambda j: (j, 0, 0)),
                   pl.BlockSpec((tt, B, C1), lambda j: (nj - 1 - j, 0, 0)),
                   pl.BlockSpec((B, C1), lambda j: (0, 0)),
                   pl.BlockSpec((B, C1), lambda j: (0, 0))],
        out_shape=[seq, seq, state, state],
        scratch_shapes=[pltpu.VMEM((2, tt + kw - 1, B, C1), f32), pltpu.VMEM((2, tt, B, C1), f32),
                        pltpu.VMEM((2, tt, B, C1), f32), pltpu.VMEM((2, tt, B, C1), f32),
                        pltpu.VMEM((2, B, C1), f32)],
        compiler_params=_params("arbitrary"),
        name="rglru",
    )(x, x, x, x, x, x, h0f, h0b, cw, cb, wbd, gb, lam)


def _block_diag_chunks(w, chunk):
    *lead, nb, bs, _ = w.shape
    per = chunk // bs
    w6 = w.reshape(*lead, nb // per, per, bs, bs)
    eye = jnp.eye(per, dtype=w.dtype)
    dense = jnp.einsum('...ckij,kl->...ckilj', w6, eye)
    return dense.reshape(*lead, nb // per, chunk, chunk)


def _to_tm(a, B, T):
    return jnp.swapaxes(a.reshape(B, T, a.shape[-1]), 0, 1)


def _to_bm(a):
    T, B, C = a.shape
    return jnp.swapaxes(a, 0, 1).reshape(B * T, C)


def _even_mixer(xs, g, mods, Ts, rows_const, w_in, sc_w, sc_b, cc_w, cc_b, ln_g, ln_b, w_out, B):
    D = w_in.shape[0]
    S = sc_w.shape[1]
    ts = _divisor(S, MXU_DIM)
    w_r = w_in.reshape(D, 5, S // ts, ts).transpose(0, 2, 1, 3).reshape(D, 5 * S).astype(bf16)
    w_o = w_out.astype(bf16)
    row1 = lambda a: a.reshape(1, S)
    outs = []
    for x, T, rc in zip(xs, Ts, rows_const):
        bg, p, q = _evenin(x, g, mods, w_r, ts, T, rc)
        outs.append(_even_out(bg, p, q, sc_w, row1(sc_b), cc_w, row1(cc_b), row1(ln_g), row1(ln_b), w_o, x, mods,
                              B, T, rc))
    return outs


def _odd_mixer_lat(xl, xc, g, mods, B, L, Cn, w_in, q_g, k_g, rpb, conv_w, conv_b, gate_w, gate_b, lam, w_out):
    D = w_in.shape[0]
    hd = q_g.shape[0]
    n_heads = rpb.shape[0]
    Dn = n_heads * hd
    C1 = conv_w.shape[1]
    w = w_in.astype(bf16)
    z = _modmm(xl, g, mods, w, L)
    zc = _modmm(xc, g, mods, w, Cn, row_const=B, cols=(Dn, 3 * Dn + C1))
    o = _attention(z.reshape(B, L, -1), zc.reshape(B, Cn, -1), q_g, k_g, rpb, B, L, Cn, n_heads, hd)
    chunk = min(MXU_DIM, C1)
    wbd = _block_diag_chunks(gate_w, chunk).astype(bf16)
    bcast = lambda a: jnp.broadcast_to(a[..., None, :], a.shape[:-1] + (B, a.shape[-1]))
    cw, cb = bcast(conv_w), bcast(conv_b)
    gb = gate_b[:, :, None, :]
    lam3 = lam[:, None, :]
    zero = jnp.zeros((B, C1), f32)
    xrc = _to_tm(zc[:, 2 * Dn:2 * Dn + C1], B, Cn)
    _, _, sf, sb = _lru(xrc, zero, zero, cw, cb, wbd, gb, lam3)
    xr = _to_tm(z[:, 3 * Dn:3 * Dn + C1], B, L)
    hf, hb, _, _ = _lru(xr, sf, sb, cw, cb, wbd, gb, lam3)
    assert Dn == C1
    return _outproj_odd(o.reshape(B * L, Dn), _to_bm(hf), _to_bm(hb), z, (3 * Dn + C1) // C1,
                        w_out.astype(bf16), xl, mods, L)


def kernel(x, c, ctx, c_ctx, w_mod, b_mod, norm_g, ffn_w_in, ffn_w_out, ev_w_in, sc_w, sc_b, cc_w, cc_b, cc_ln_g, cc_ln_b, ev_w_out, od_w_in, q_norm_g, k_norm_g, na_rpb, lru_conv_w, lru_conv_b, lru_gate_w, lru_gate_b, lru_lam, od_w_out):
    B, L, D = x.shape
    Cn = ctx.shape[1]
    depth = w_mod.shape[0]
    R = -(-(B + 1) // 8) * 8
    cvec = jnp.zeros((R, D), f32).at[:B].set(c).at[B].set(c_ctx)
    mods_all = _mod_table(cvec, w_mod, b_mod)
    xl = x.reshape(B * L, D)
    xc = ctx.reshape(B * Cn, D)
    wi, wo = ffn_w_in.astype(bf16), ffn_w_out.astype(bf16)
    for l in range(depth):
        last = l == depth - 1
        odd = l % 2 == 1
        j = l // 2
        ctx_in = odd or not last
        ctx_out = not last
        mods = mods_all[l].reshape(R, 1, N_MOD * D)
        g = norm_g[l].reshape(3, 1, D)
        xl = _ffn(xl, g[0], mods, 0, wi, wo, (l, 0), L)
        if ctx_in:
            xc = _ffn(xc, g[0], mods, 0, wi, wo, (l, 0), Cn, row_const=B)
        if odd:
            assert ctx_in and not ctx_out
            xl = _odd_mixer_lat(xl, xc, g[1], mods, B, L, Cn, od_w_in[j], q_norm_g[j], k_norm_g[j], na_rpb[j],
                                lru_conv_w[j], lru_conv_b[j], lru_gate_w[j], lru_gate_b[j], lru_lam[j],
                                od_w_out[j])
        else:
            streams = [(xl, L, None)] + ([(xc, Cn, B)] if ctx_out else [])
            outs = _even_mixer([s[0] for s in streams], g[1], mods, [s[1] for s in streams],
                               [s[2] for s in streams], ev_w_in[j], sc_w[j], sc_b[j], cc_w[j], cc_b[j],
                               cc_ln_g[j], cc_ln_b[j], ev_w_out[j], B)
            xl = outs[0]
            if ctx_out:
                xc = outs[1]
        xl = _ffn(xl, g[2], mods, 6, wi, wo, (l, 1), L)
        if ctx_out:
            xc = _ffn(xc, g[2], mods, 6, wi, wo, (l, 1), Cn, row_const=B)
    return xl.reshape(B, L, D)
```

```python
import functools
import math

import jax
import jax.numpy as jnp
from jax import lax
from jax.experimental import pallas as pl
from jax.experimental.pallas import tpu as pltpu

f32 = jnp.float32
bf16 = jnp.bfloat16

GRID_W = 64
LRU_C = 8.0
EPS = 1e-6
NEG_INF = -1e30
N_MOD = 9
MXU_DIM = 256
VMEM_LIMIT_BYTES = 60 * 1024 * 1024


def _divisor(n, pref):
    d = min(n, pref)
    while n % d:
        d -= 1
    return d


def _params(*sem):
    return pltpu.CompilerParams(dimension_semantics=sem, vmem_limit_bytes=VMEM_LIMIT_BYTES)


def _row_tiling(M, T, row_const, pref):
    if row_const is not None:
        return _divisor(M, pref), (lambda i: row_const)
    tm = _divisor(T, pref)
    nt = T // tm
    return tm, (lambda i: i // nt)


def _silu(x):
    return x * jax.nn.sigmoid(x)


def _gelu_tanh(x):
    return 0.5 * x * (1.0 + jnp.tanh(0.7978845608028654 * (x + 0.044715 * (x * x * x))))


MOD_ROWS = 16
MOD_CHUNKS = 8


def _modulate_rows(x_ref, g_ref, sh_ref, sc_ref, hm_ref, zero_ref=None):
    tm, D = x_ref.shape
    rc = _divisor(tm, MOD_ROWS)
    per_trip = _divisor(tm // rc, MOD_CHUNKS)
    gain = g_ref[...] * (1.0 + sc_ref[0])
    shift = sh_ref[0]

    def body(i, carry):
        for sub in range(per_trip):
            r0 = pl.multiple_of((i * per_trip + sub) * rc, rc)
            x = x_ref[pl.ds(r0, rc), :]
            ms = jnp.mean(x * x, axis=-1, keepdims=True)
            hm_ref[pl.ds(r0, rc), :] = (x * lax.rsqrt(ms + EPS) * gain + shift).astype(bf16)
            if zero_ref is not None:
                zero_ref[pl.ds(r0, rc), :] = jnp.zeros((rc, D), zero_ref.dtype)
        return carry

    lax.fori_loop(0, tm // (rc * per_trip), body, 0)


def _dot(a, b):
    return jnp.dot(a, b, preferred_element_type=f32)


def _mod_spec(D, row, k):
    return pl.BlockSpec((1, 1, D), lambda i, j: (row(i), 0, k))


def _mod_kernel(c_ref, w_ref, b_ref, o_ref):
    s = _silu(c_ref[...]).astype(bf16)
    o_ref[0] = _dot(s, w_ref[0].astype(bf16)) + b_ref[0]


def _mod_table(cvec, w_mod, b_mod):
    depth, D, N = w_mod.shape
    R = cvec.shape[0]
    tn = _divisor(N, 1024)
    return pl.pallas_call(
        _mod_kernel,
        grid=(depth, N // tn),
        in_specs=[pl.BlockSpec((R, D), lambda l, n: (0, 0)),
                  pl.BlockSpec((1, D, tn), lambda l, n: (l, 0, n)),
                  pl.BlockSpec((1, 1, tn), lambda l, n: (l, 0, n))],
        out_specs=pl.BlockSpec((1, R, tn), lambda l, n: (l, 0, n)),
        out_shape=jax.ShapeDtypeStruct((depth, R, N), f32),
        compiler_params=_params("arbitrary", "arbitrary"),
        name="mod_table",
    )(cvec, w_mod, b_mod.reshape(depth, 1, N))


def _ffn_kernel(x_ref, g_ref, sh_ref, sc_ref, gt_ref, wg_ref, wu_ref, wo_ref, o_ref, hm_ref, *, nf):
    f = pl.program_id(1)

    @pl.when(f == 0)
    def _():
        _modulate_rows(x_ref, g_ref, sh_ref, sc_ref, hm_ref, zero_ref=o_ref)

    h = hm_ref[...]
    gate = _dot(h, wg_ref[...])
    up = _dot(h, wu_ref[...])
    o_ref[...] += _dot((_silu(gate) * up).astype(bf16), wo_ref[...])

    @pl.when(f == nf - 1)
    def _():
        o_ref[...] = x_ref[...] + 0.5 * gt_ref[0] * o_ref[...]


FFN_HIDDEN_TILE = 512


def _ffn(x, g, mods, k0, w_in, w_out, which, T, row_const=None):
    M, D = x.shape
    l, a = which
    F = w_out.shape[2]
    tf = _divisor(F, FFN_HIDDEN_TILE)
    nf = F // tf
    tm, row = _row_tiling(M, T, row_const, 1024)
    return pl.pallas_call(
        functools.partial(_ffn_kernel, nf=nf),
        grid=(M // tm, nf),
        in_specs=[pl.BlockSpec((tm, D), lambda i, f: (i, 0)),
                  pl.BlockSpec((1, D), lambda i, f: (0, 0)),
                  _mod_spec(D, row, k0), _mod_spec(D, row, k0 + 1), _mod_spec(D, row, k0 + 2),
                  pl.BlockSpec((None, None, D, tf), lambda i, f: (l, a, 0, f)),
                  pl.BlockSpec((None, None, D, tf), lambda i, f: (l, a, 0, nf + f)),
                  pl.BlockSpec((None, None, tf, D), lambda i, f: (l, a, f, 0))],
        out_specs=pl.BlockSpec((tm, D), lambda i, f: (i, 0)),
        out_shape=jax.ShapeDtypeStruct((M, D), f32),
        scratch_shapes=[pltpu.VMEM((tm, D), bf16)],
        compiler_params=_params("arbitrary", "arbitrary"),
        name="ffn",
    )(x, g, mods, mods, mods, w_in, w_in, w_out)


def _modulate_once(x_ref, g_ref, sh_ref, sc_ref, hm_ref):
    @pl.when(pl.program_id(1) == 0)
    def _():
        _modulate_rows(x_ref, g_ref, sh_ref, sc_ref, hm_ref)


def _modmm_kernel(x_ref, g_ref, sh_ref, sc_ref, w_ref, o_ref, hm_ref):
    _modulate_once(x_ref, g_ref, sh_ref, sc_ref, hm_ref)
    o_ref[...] = _dot(hm_ref[...], w_ref[...])


def _inproj_specs(D, tm, wn, row, n0=0):
    return [pl.BlockSpec((tm, D), lambda i, n: (i, 0)),
            pl.BlockSpec((1, D), lambda i, n: (0, 0)),
            _mod_spec(D, row, 3), _mod_spec(D, row, 4),
            pl.BlockSpec((D, wn), lambda i, n: (0, n0 + n))]


def _modmm(x, g, mods, w, T, row_const=None, cols=None):
    M, D = x.shape
    c0, c1 = cols if cols is not None else (0, w.shape[1])
    N = c1 - c0
    tm, row = _row_tiling(M, T, row_const, 1024)
    tn = _divisor(math.gcd(N, c0), 1024)
    return pl.pallas_call(
        _modmm_kernel,
        grid=(M // tm, N // tn),
        in_specs=_inproj_specs(D, tm, tn, row, c0 // tn),
        out_specs=pl.BlockSpec((tm, tn), lambda i, n: (i, n)),
        out_shape=jax.ShapeDtypeStruct((M, N), f32),
        scratch_shapes=[pltpu.VMEM((tm, D), bf16)],
        compiler_params=_params("arbitrary", "arbitrary"),
        name="modmm",
    )(x, g, mods, mods, w)


def _evenin_kernel(x_ref, g_ref, sh_ref, sc_ref, w_ref, bg_ref, p_ref, q_ref, hm_ref, *, ts):
    _modulate_once(x_ref, g_ref, sh_ref, sc_ref, hm_ref)
    z = _dot(hm_ref[...], w_ref[...])
    bg_ref[...] = z[:, 0:ts]
    p_ref[...] = z[:, ts:2 * ts] * z[:, 2 * ts:3 * ts]
    q_ref[...] = z[:, 3 * ts:4 * ts] * jax.nn.sigmoid(z[:, 4 * ts:5 * ts])


def _evenin(x, g, mods, w_r, ts, T, row_const=None):
    M, D = x.shape
    S = w_r.shape[1] // 5
    tm, row = _row_tiling(M, T, row_const, 1024)
    out = jax.ShapeDtypeStruct((M, S), f32)
    ospec = pl.BlockSpec((tm, ts), lambda i, n: (i, n))
    return pl.pallas_call(
        functools.partial(_evenin_kernel, ts=ts),
        grid=(M // tm, S // ts),
        in_specs=_inproj_specs(D, tm, 5 * ts, row),
        out_specs=[ospec, ospec, ospec],
        out_shape=[out, out, out],
        scratch_shapes=[pltpu.VMEM((tm, D), bf16)],
        compiler_params=_params("arbitrary", "arbitrary"),
        name="evenin",
    )(x, g, mods, mods, w_r)


def _shifted_taps(win, w_ref, cs, lo, halo, tm):
    acc = None
    for s in range(8):
        taps = [k for k in range(w_ref.shape[0]) if (halo + k - lo + s) % 8 == 0]
        if not taps:
            continue
        wr = win if s == 0 else pltpu.roll(win, s, axis=0)
        for k in taps:
            st = halo + k - lo + s
            term = w_ref[k:k + 1, cs] * wr[st:st + tm]
            acc = term if acc is None else acc + term
    return acc


def _even_out_kernel(bg_ref, pp_ref, p_ref, pn_ref, qp_ref, q_ref, qn_ref, scw_ref, scb_ref, ccw_ref, ccb_ref,
                     lng_ref, lnb_ref, w_ref, x_ref, gt_ref, o_ref, ysc_ref, u_ref, *, tm):
    S = bg_ref.shape[1]
    hp, hq = pp_ref.shape[0], qp_ref.shape[0]
    t = pl.program_id(1)
    first = t == 0
    last = t == pl.num_programs(1) - 1
    lo_sc = (scw_ref.shape[0] - 1) // 2
    lo_cc = (ccw_ref.shape[0] - 1) // 2
    for c in range(S // 128):
        cs = slice(c * 128, (c + 1) * 128)
        pw = jnp.concatenate([jnp.where(first, 0.0, pp_ref[:, cs]), p_ref[:, cs],
                              jnp.where(last, 0.0, pn_ref[:, cs])], axis=0)
        conv = _shifted_taps(pw, scw_ref, cs, lo_sc, hp, tm)
        ysc_ref[:, cs] = (bg_ref[:, cs] * (conv + scb_ref[:, cs])).astype(bf16)
    y = _dot(ysc_ref[...], w_ref[0:S, :])
    for c in range(S // 128):
        cs = slice(c * 128, (c + 1) * 128)
        qw = jnp.concatenate([jnp.where(first, 0.0, qp_ref[:, cs]), q_ref[:, cs],
                              jnp.where(last, 0.0, qn_ref[:, cs])], axis=0)
        u_ref[:, cs] = _shifted_taps(qw, ccw_ref, cs, lo_cc, hq, tm) + ccb_ref[:, cs]
    u = u_ref[...]
    mu = jnp.mean(u, axis=-1, keepdims=True)
    xc = u - mu
    var = jnp.mean(xc * xc, axis=-1, keepdims=True)
    ycc = _silu(xc * lax.rsqrt(var + EPS) * lng_ref[...] + lnb_ref[...]).astype(bf16)
    y = y + _dot(ycc, w_ref[S:2 * S, :])
    o_ref[...] = x_ref[...] + gt_ref[0] * y


def _even_out(bg, p, q, scw, scb, ccw, ccb, lng, lnb, w, x, mods, B, T, row_const=None):
    M, D = x.shape
    S = bg.shape[1]
    hp, hq = 8, 16
    assert (scw.shape[0] - 1) // 2 <= hp and (ccw.shape[0] - 1) // 2 < hq
    tm = _divisor(T, 256)
    nt = T // tm
    rp, rq = tm // hp, tm // hq
    tile = lambda b, t: b * nt + t
    cur = pl.BlockSpec((tm, S), lambda b, t: (tile(b, t), 0))
    prev = lambda h, r: pl.BlockSpec((h, S), lambda b, t: (jnp.maximum(tile(b, t) * r - 1, 0), 0))
    nxt = lambda h, r: pl.BlockSpec((h, S), lambda b, t: (jnp.minimum((tile(b, t) + 1) * r, M // h - 1), 0))
    full = lambda a: pl.BlockSpec(a.shape, lambda b, t: (0,) * a.ndim)
    row = (lambda b: row_const) if row_const is not None else (lambda b: b)
    return pl.pallas_call(
        functools.partial(_even_out_kernel, tm=tm),
        grid=(B, nt),
        in_specs=[cur, prev(hp, rp), cur, nxt(hp, rp), prev(hq, rq), cur, nxt(hq, rq),
                  full(scw), full(scb), full(ccw), full(ccb), full(lng), full(lnb), full(w),
                  pl.BlockSpec((tm, D), lambda b, t: (tile(b, t), 0)),
                  pl.BlockSpec((1, 1, D), lambda b, t: (row(b), 0, 5))],
        out_specs=pl.BlockSpec((tm, D), lambda b, t: (tile(b, t), 0)),
        out_shape=jax.ShapeDtypeStruct((M, D), f32),
        scratch_shapes=[pltpu.VMEM((tm, S), bf16), pltpu.VMEM((tm, S), f32)],
        compiler_params=_params("arbitrary", "arbitrary"),
        name="even_out",
    )(bg, p, p, p, q, q, q, scw, scb, ccw, ccb, lng, lnb, w, x, mods)


def _outproj_odd_kernel(o_ref, hf_ref, hb_ref, gr_ref, wa_ref, wb_ref, x_ref, gt_ref, out_ref):
    r = ((hf_ref[...] + hb_ref[...]) * _gelu_tanh(gr_ref[...])).astype(bf16)
    y = _dot(o_ref[...], wa_ref[...]) + _dot(r, wb_ref[...])
    out_ref[...] = x_ref[...] + gt_ref[0] * y


def _outproj_odd(o, hf, hb, z, gr_block, w, x, mods, T):
    M, D = x.shape
    Ka, Kb = o.shape[1], hf.shape[1]
    tm = _divisor(T, 256)
    nt = T // tm
    return pl.pallas_call(
        _outproj_odd_kernel,
        grid=(M // tm,),
        in_specs=[pl.BlockSpec((tm, Ka), lambda i: (i, 0)),
                  pl.BlockSpec((tm, Kb), lambda i: (i, 0)),
                  pl.BlockSpec((tm, Kb), lambda i: (i, 0)),
                  pl.BlockSpec((tm, Kb), lambda i: (i, gr_block)),
                  pl.BlockSpec((Ka, D), lambda i: (0, 0)),
                  pl.BlockSpec((Kb, D), lambda i: (Ka // Kb, 0)),
                  pl.BlockSpec((tm, D), lambda i: (i, 0)),
                  pl.BlockSpec((1, 1, D), lambda i: (i // nt, 0, 5))],
        out_specs=pl.BlockSpec((tm, D), lambda i: (i, 0)),
        out_shape=jax.ShapeDtypeStruct((M, D), f32),
        compiler_params=_params("arbitrary"),
        name="outproj_odd",
    )(o, hf, hb, z, w, w, x, mods)


def _attn_kernel(q_ref, k_ref, v_ref, kc_ref, vc_ref, qg_ref, kg_ref, bias_ref, o_ref,
                 qn_ref, kn_ref, vn_ref, kcn_ref, vcn_ref, *, rows, kr, win_rows, hd, scale):
    rows_per_trip = _divisor(rows, 8)
    lanes = 2 * hd
    W = GRID_W
    ri = jnp.where(lax.broadcasted_iota(jnp.int32, (lanes, lanes), 0) < hd, 1.0, -1.0)
    ci = jnp.where(lax.broadcasted_iota(jnp.int32, (lanes, lanes), 1) < hd, 1.0, -1.0)
    seg_mean = jnp.where(ri * ci > 0.0, 1.0 / hd, 0.0).astype(bf16)

    def head_norm(x, g):
        sq = x * x
        hi = sq.astype(bf16)
        lo = (sq - hi.astype(f32)).astype(bf16)
        ms = _dot(hi, seg_mean) + _dot(lo, seg_mean)
        return x * lax.rsqrt(ms + EPS) * g

    head0 = lax.broadcasted_iota(jnp.int32, (1, lanes), 1) < hd
    qn = head_norm(q_ref[0], qg_ref[...]) * scale
    q_h0 = jnp.where(head0, qn, 0.0).astype(bf16)
    q_h1 = jnp.where(head0, 0.0, qn).astype(bf16)
    for r in range(rows):
        qn_ref[2 * r * W:(2 * r + 1) * W] = q_h0[r * W:(r + 1) * W]
        qn_ref[(2 * r + 1) * W:(2 * r + 2) * W] = q_h1[r * W:(r + 1) * W]
    kn_ref[...] = head_norm(k_ref[0], kg_ref[...]).astype(bf16)
    kcn_ref[...] = head_norm(kc_ref[0], kg_ref[...]).astype(bf16)
    vn_ref[...] = v_ref[0].astype(bf16)
    vcn_ref[...] = vc_ref[0].astype(bf16)
    nt_dims = (((1,), (1,)), ((), ()))
    span = 2 * W
    trip = rows_per_trip * span

    def body(it, carry):
        q_all = qn_ref[pl.ds(pl.multiple_of(it * trip, trip), trip), :]
        sc_all = lax.dot_general(q_all, kcn_ref[...], nt_dims, preferred_element_type=f32)
        plan, local = [], []
        for sub in range(rows_per_trip):
            r = it * rows_per_trip + sub
            r0 = jnp.clip(r - kr // 2, 0, rows - kr)
            oi = r - r0
            ks = pl.multiple_of(r0 * W, W)
            plan.append((r, ks))
            bias = jnp.concatenate([bias_ref[0, 2 * jj - oi + win_rows - 1] for jj in range(kr // 2)], axis=1)
            kw = kn_ref[pl.ds(ks, kr * W), :]
            local.append(lax.dot_general(q_all[sub * span:(sub + 1) * span], kw, nt_dims,
                                         preferred_element_type=f32) + bias)
        p_local, p_ctx, dens = [], [], []
        for sub in range(rows_per_trip):
            sl = local[sub]
            sc = sc_all[sub * span:(sub + 1) * span]
            m = jnp.maximum(jnp.max(sl, axis=-1, keepdims=True), jnp.max(sc, axis=-1, keepdims=True))
            el = jnp.exp(sl - m)
            ec = jnp.exp(sc - m)
            dens.append(jnp.sum(el, axis=-1, keepdims=True) + jnp.sum(ec, axis=-1, keepdims=True))
            p_local.append(el.astype(bf16))
            p_ctx.append(ec.astype(bf16))
        o_ctx = _dot(jnp.concatenate(p_ctx, axis=0), vcn_ref[...])
        for sub, (r, ks) in enumerate(plan):
            vw = vn_ref[pl.ds(ks, kr * W), :]
            o2 = (_dot(p_local[sub], vw) + o_ctx[sub * span:(sub + 1) * span]) / dens[sub]
            o_ref[0, pl.ds(pl.multiple_of(r * W, W), W), :] = jnp.where(head0, o2[:W], o2[W:]).astype(o_ref.dtype)
        return carry

    lax.fori_loop(0, rows // rows_per_trip, body, 0)


def _bias_table(rpb):
    H, nr, nc = rpb.shape
    Wc = (nc + 1) // 2
    W = GRID_W
    r = rpb.astype(f32)
    period = jnp.concatenate([r[:, :, Wc - 1:], jnp.zeros((H, nr, W + 1 - nc), f32), r[:, :, :Wc - 1]], axis=-1)
    toeplitz = jnp.tile(period, (1, 1, W))[:, :, :W * W].reshape(H, nr, W, W)
    col = jnp.arange(W)
    col_start = jnp.clip(col - Wc // 2, 0, W - Wc)
    col_in = (col[None, :] >= col_start[:, None]) & (col[None, :] < col_start[:, None] + Wc)
    bias_cols = jnp.where(col_in, toeplitz, NEG_INF)
    return jnp.concatenate([bias_cols[:, :-1], bias_cols[:, 1:]], axis=-1)


def _attention(z, zc, qg, kg, rpb, B, L, Cn, n_heads, hd):
    rows = L // GRID_W
    win_rows = (rpb.shape[1] + 1) // 2
    kr = min(win_rows, rows)
    assert kr % 2 == 0
    npair = n_heads // 2
    lanes = 2 * hd
    nd = 2 * win_rows - 2
    bias = _bias_table(rpb).reshape(npair, 2, nd, GRID_W, 2 * GRID_W).transpose(0, 2, 1, 3, 4)
    bias = bias.reshape(npair, nd, 2 * GRID_W, 2 * GRID_W)
    qg2 = jnp.tile(qg, 2).reshape(1, lanes)
    kg2 = jnp.tile(kg, 2).reshape(1, lanes)
    seg = lambda s: pl.BlockSpec((1, L, lanes), lambda p, b: (b, 0, s * npair + p))
    segc = lambda s: pl.BlockSpec((1, Cn, lanes), lambda p, b: (b, 0, s * npair + p))
    return pl.pallas_call(
        functools.partial(_attn_kernel, rows=rows, kr=kr, win_rows=win_rows, hd=hd, scale=hd ** -0.5),
        grid=(npair, B),
        in_specs=[seg(0), seg(1), seg(2), segc(0), segc(1),
                  pl.BlockSpec((1, lanes), lambda p, b: (0, 0)),
                  pl.BlockSpec((1, lanes), lambda p, b: (0, 0)),
                  pl.BlockSpec((1, nd, 2 * GRID_W, 2 * GRID_W), lambda p, b: (p, 0, 0, 0))],
        out_specs=pl.BlockSpec((1, L, lanes), lambda p, b: (b, 0, p)),
        out_shape=jax.ShapeDtypeStruct((B, L, npair * lanes), bf16),
        scratch_shapes=[pltpu.VMEM((2 * L, lanes), bf16), pltpu.VMEM((L, lanes), bf16),
                        pltpu.VMEM((L, lanes), bf16), pltpu.VMEM((Cn, lanes), bf16),
                        pltpu.VMEM((Cn, lanes), bf16)],
        compiler_params=_params("arbitrary", "arbitrary"),
        name="attention",
    )(z, z, z, zc, zc, qg2, kg2, bias)


def _lru_kernel(xfp_ref, xf_ref, xfn_ref, xbp_ref, xb_ref, xbn_ref, h0f_ref, h0b_ref, cw_ref, cb_ref,
                wbd_ref, gb_ref, lam_ref, hf_ref, hb_ref, sf_ref, sb_ref,
                xw_ref, xc_ref, a_ref, u_ref, st_ref, *, tt, tc, nj, chunk):
    B, C1 = xf_ref.shape[1], xf_ref.shape[2]
    kw = cw_ref.shape[0]
    lo = (kw - 1) // 2
    hi = kw - 1 - lo
    j = pl.program_id(0)

    @pl.when(j == 0)
    def _():
        st_ref[0] = h0f_ref[...]
        st_ref[1] = h0b_ref[...]

    for d, (xp_ref, xm_ref, xn_ref, jj) in enumerate(((xfp_ref, xf_ref, xfn_ref, j),
                                                     (xbp_ref, xb_ref, xbn_ref, nj - 1 - j))):
        xw_ref[d, 0:lo] = jnp.where(jj == 0, 0.0, xp_ref[...])
        xw_ref[d, lo:lo + tt] = xm_ref[...]
        xw_ref[d, lo + tt:lo + tt + hi] = jnp.where(jj == nj - 1, 0.0, xn_ref[...])

        def conv_body(ci, carry, d=d):
            t0 = ci * tc
            acc = cw_ref[0][None] * xw_ref[d, pl.ds(t0, tc)]
            for k in range(1, kw):
                acc = acc + cw_ref[k][None] * xw_ref[d, pl.ds(t0 + k, tc)]
            xc_ref[d, pl.ds(t0, tc)] = acc + cb_ref[...][None]
            return carry

        lax.fori_loop(0, tt // tc, conv_body, 0)

        lam = lam_ref[d]
        neg = -lam
        softplus = jnp.maximum(neg, 0.0) + jnp.log(1.0 + jnp.exp(-jnp.abs(neg)))
        rate = -LRU_C * softplus
        for c in range(C1 // chunk):
            cs = slice(c * chunk, (c + 1) * chunk)
            xs = xc_ref[d, :, :, cs].reshape(tt * B, chunk)
            xh = xs.astype(bf16)
            r = jax.nn.sigmoid(_dot(xh, wbd_ref[d, 0, c]) + gb_ref[d, 0][:, cs])
            i = jax.nn.sigmoid(_dot(xh, wbd_ref[d, 1, c]) + gb_ref[d, 1][:, cs])
            log_a = rate[:, cs] * r
            a = jnp.exp(log_a)
            mult = jnp.sqrt(jnp.tanh(-log_a) * (a * a + 1.0))
            a_ref[d, :, :, cs] = a.reshape(tt, B, chunk)
            u_ref[d, :, :, cs] = (mult * i * xs).reshape(tt, B, chunk)

    def scan_body(s, carry):
        hf, hb = carry
        tb = tt - 1 - s
        hf = a_ref[0, s] * hf + u_ref[0, s]
        hb = a_ref[1, tb] * hb + u_ref[1, tb]
        hf_ref[s] = hf
        hb_ref[tb] = hb
        return hf, hb

    hf, hb = lax.fori_loop(0, tt, scan_body, (st_ref[0], st_ref[1]))
    st_ref[0] = hf
    st_ref[1] = hb
    sf_ref[...] = hf
    sb_ref[...] = hb


def _lru(x, h0f, h0b, cw, cb, wbd, gb, lam):
    T, B, C1 = x.shape
    kw = cw.shape[0]
    lo = (kw - 1) // 2
    hi = kw - 1 - lo
    assert lo == 1 and hi == 2
    chunk = wbd.shape[-1]
    tt = _divisor(T, 64)
    assert tt % hi == 0
    nj = T // tt
    full = lambda a: pl.BlockSpec(a.shape, lambda j: (0,) * a.ndim)
    state = jax.ShapeDtypeStruct((B, C1), f32)
    seq = jax.ShapeDtypeStruct((T, B, C1), f32)

    def windows(chunk_of):
        return [pl.BlockSpec((lo, B, C1), lambda j: (jnp.maximum(chunk_of(j) * tt - 1, 0), 0, 0)),
                pl.BlockSpec((tt, B, C1), lambda j: (chunk_of(j), 0, 0)),
                pl.BlockSpec((hi, B, C1),
                             lambda j: (jnp.minimum((chunk_of(j) + 1) * (tt // hi), T // hi - 1), 0, 0))]

    fwd = lambda j: j
    bwd = lambda j: nj - 1 - j
    return pl.pallas_call(
        functools.partial(_lru_kernel, tt=tt, tc=8, nj=nj, chunk=chunk),
        grid=(nj,),
        in_specs=windows(fwd) + windows(bwd) + [full(h0f), full(h0b), full(cw), full(cb), full(wbd),
                                                full(gb), full(lam)],
        out_specs=[pl.BlockSpec((tt, B, C1), lambda j: (j, 0, 0)),
                   pl.BlockSpec((tt, B, C1), lambda j: (nj - 1 - j, 0, 0)),
                   pl.BlockSpec((B, C1), lambda j: (0, 0)),
                   pl.BlockSpec((B, C1), lambda j: (0, 0))],
        out_shape=[seq, seq, state, state],
        scratch_shapes=[pltpu.VMEM((2, tt + kw - 1, B, C1), f32), pltpu.VMEM((2, tt, B, C1), f32),
                        pltpu.VMEM((2, tt, B, C1), f32), pltpu.VMEM((2, tt, B, C1), f32),
                        pltpu.VMEM((2, B, C1), f32)],
        compiler_params=_params("arbitrary"),
        name="rglru",
    )(x, x, x, x, x, x, h0f, h0b, cw, cb, wbd, gb, lam)


def _block_diag_chunks(w, chunk):
    *lead, nb, bs, _ = w.shape
    per = chunk // bs
    w6 = w.reshape(*lead, nb // per, per, bs, bs)
    eye = jnp.eye(per, dtype=w.dtype)
    dense = jnp.einsum('...ckij,kl->...ckilj', w6, eye)
    return dense.reshape(*lead, nb // per, chunk, chunk)


def _to_tm(a, B, T):
    return jnp.swapaxes(a.reshape(B, T, a.shape[-1]), 0, 1)


def _to_bm(a):
    T, B, C = a.shape
    return jnp.swapaxes(a, 0, 1).reshape(B * T, C)


def _even_mixer(xs, g, mods, Ts, rows_const, w_in, sc_w, sc_b, cc_w, cc_b, ln_g, ln_b, w_out, B):
    D = w_in.shape[0]
    S = sc_w.shape[1]
    ts = _divisor(S, MXU_DIM)
    w_r = w_in.reshape(D, 5, S // ts, ts).transpose(0, 2, 1, 3).reshape(D, 5 * S).astype(bf16)
    w_o = w_out.astype(bf16)
    row1 = lambda a: a.reshape(1, S)
    outs = []
    for x, T, rc in zip(xs, Ts, rows_const):
        bg, p, q = _evenin(x, g, mods, w_r, ts, T, rc)
        outs.append(_even_out(bg, p, q, sc_w, row1(sc_b), cc_w, row1(cc_b), row1(ln_g), row1(ln_b), w_o, x, mods,
                              B, T, rc))
    return outs


def _odd_mixer_lat(xl, xc, g, mods, B, L, Cn, w_in, q_g, k_g, rpb, conv_w, conv_b, gate_w, gate_b, lam, w_out):
    D = w_in.shape[0]
    hd = q_g.shape[0]
    n_heads = rpb.shape[0]
    Dn = n_heads * hd
    C1 = conv_w.shape[1]
    w = w_in.astype(bf16)
    z = _modmm(xl, g, mods, w, L)
    zc = _modmm(xc, g, mods, w, Cn, row_const=B, cols=(Dn, 3 * Dn + C1))
    o = _attention(z.reshape(B, L, -1), zc.reshape(B, Cn, -1), q_g, k_g, rpb, B, L, Cn, n_heads, hd)
    chunk = min(MXU_DIM, C1)
    wbd = _block_diag_chunks(gate_w, chunk).astype(bf16)
    bcast = lambda a: jnp.broadcast_to(a[..., None, :], a.shape[:-1] + (B, a.shape[-1]))
    cw, cb = bcast(conv_w), bcast(conv_b)
    gb = gate_b[:, :, None, :]
    lam3 = lam[:, None, :]
    zero = jnp.zeros((B, C1), f32)
    xrc = _to_tm(zc[:, 2 * Dn:2 * Dn + C1], B, Cn)
    _, _, sf, sb = _lru(xrc, zero, zero, cw, cb, wbd, gb, lam3)
    xr = _to_tm(z[:, 3 * Dn:3 * Dn + C1], B, L)
    hf, hb, _, _ = _lru(xr, sf, sb, cw, cb, wbd, gb, lam3)
    assert Dn == C1
    return _outproj_odd(o.reshape(B * L, Dn), _to_bm(hf), _to_bm(hb), z, (3 * Dn + C1) // C1,
                        w_out.astype(bf16), xl, mods, L)


def kernel(x, c, ctx, c_ctx, w_mod, b_mod, norm_g, ffn_w_in, ffn_w_out, ev_w_in, sc_w, sc_b, cc_w, cc_b, cc_ln_g, cc_ln_b, ev_w_out, od_w_in, q_norm_g, k_norm_g, na_rpb, lru_conv_w, lru_conv_b, lru_gate_w, lru_gate_b, lru_lam, od_w_out):
    B, L, D = x.shape
    Cn = ctx.shape[1]
    depth = w_mod.shape[0]
    R = -(-(B + 1) // 8) * 8
    cvec = jnp.zeros((R, D), f32).at[:B].set(c).at[B].set(c_ctx)
    mods_all = _mod_table(cvec, w_mod, b_mod)
    xl = x.reshape(B * L, D)
    xc = ctx.reshape(B * Cn, D)
    wi, wo = ffn_w_in.astype(bf16), ffn_w_out.astype(bf16)
    for l in range(depth):
        last = l == depth - 1
        odd = l % 2 == 1
        j = l // 2
        ctx_in = odd or not last
        ctx_out = not last
        mods = mods_all[l].reshape(R, 1, N_MOD * D)
        g = norm_g[l].reshape(3, 1, D)
        xl = _ffn(xl, g[0], mods, 0, wi, wo, (l, 0), L)
        if ctx_in:
            xc = _ffn(xc, g[0], mods, 0, wi, wo, (l, 0), Cn, row_const=B)
        if odd:
            assert ctx_in and not ctx_out
            xl = _odd_mixer_lat(xl, xc, g[1], mods, B, L, Cn, od_w_in[j], q_norm_g[j], k_norm_g[j], na_rpb[j],
                                lru_conv_w[j], lru_conv_b[j], lru_gate_w[j], lru_gate_b[j], lru_lam[j],
                                od_w_out[j])
        else:
            streams = [(xl, L, None)] + ([(xc, Cn, B)] if ctx_out else [])
            outs = _even_mixer([s[0] for s in streams], g[1], mods, [s[1] for s in streams],
                               [s[2] for s in streams], ev_w_in[j], sc_w[j], sc_b[j], cc_w[j], cc_b[j],
                               cc_ln_g[j], cc_ln_b[j], ev_w_out[j], B)
            xl = outs[0]
            if ctx_out:
                xc = outs[1]
        xl = _ffn(xl, g[2], mods, 6, wi, wo, (l, 1), L)
        if ctx_out:
            xc = _ffn(xc, g[2], mods, 6, wi, wo, (l, 1), Cn, row_const=B)
    return xl.reshape(B, L, D)
```

```python
import functools
import math

import jax
import jax.numpy as jnp
from jax import lax
from jax.experimental import pallas as pl
from jax.experimental.pallas import tpu as pltpu

f32 = jnp.float32
bf16 = jnp.bfloat16

GRID_W = 64
LRU_C = 8.0
EPS = 1e-6
NEG_INF = -1e30
N_MOD = 9
MXU_DIM = 256
VMEM_LIMIT_BYTES = 60 * 1024 * 1024


def _divisor(n, pref):
    d = min(n, pref)
    while n % d:
        d -= 1
    return d


def _params(*sem):
    return pltpu.CompilerParams(dimension_semantics=sem, vmem_limit_bytes=VMEM_LIMIT_BYTES)


def _row_tiling(M, T, row_const, pref):
    if row_const is not None:
        return _divisor(M, pref), (lambda i: row_const)
    tm = _divisor(T, pref)
    nt = T // tm
    return tm, (lambda i: i // nt)


def _sigmoid(x):
    return 0.5 * jnp.tanh(0.5 * x) + 0.5


def _silu(x):
    return x * jax.nn.sigmoid(x)


def _gelu_tanh(x):
    return 0.5 * x * (1.0 + jnp.tanh(0.7978845608028654 * (x + 0.044715 * (x * x * x))))


MOD_ROWS = 16
MOD_CHUNKS = 8


def _modulate_rows(x_ref, g_ref, sh_ref, sc_ref, hm_ref, zero_ref=None):
    tm, D = x_ref.shape
    rc = _divisor(tm, MOD_ROWS)
    per_trip = _divisor(tm // rc, MOD_CHUNKS)
    gain = g_ref[...] * (1.0 + sc_ref[0])
    shift = sh_ref[0]

    def body(i, carry):
        for sub in range(per_trip):
            r0 = pl.multiple_of((i * per_trip + sub) * rc, rc)
            x = x_ref[pl.ds(r0, rc), :]
            ms = jnp.mean(x * x, axis=-1, keepdims=True)
            hm_ref[pl.ds(r0, rc), :] = (x * lax.rsqrt(ms + EPS) * gain + shift).astype(bf16)
            if zero_ref is not None:
                zero_ref[pl.ds(r0, rc), :] = jnp.zeros((rc, D), zero_ref.dtype)
        return carry

    lax.fori_loop(0, tm // (rc * per_trip), body, 0)


def _dot(a, b):
    return jnp.dot(a, b, preferred_element_type=f32)


def _mod_spec(D, row, k):
    return pl.BlockSpec((1, 1, D), lambda i, j: (row(i), 0, k))


def _mod_kernel(c_ref, w_ref, b_ref, o_ref):
    s = _silu(c_ref[...]).astype(bf16)
    o_ref[0] = _dot(s, w_ref[0].astype(bf16)) + b_ref[0]


def _mod_table(cvec, w_mod, b_mod):
    depth, D, N = w_mod.shape
    R = cvec.shape[0]
    tn = _divisor(N, 1024)
    return pl.pallas_call(
        _mod_kernel,
        grid=(depth, N // tn),
        in_specs=[pl.BlockSpec((R, D), lambda l, n: (0, 0)),
                  pl.BlockSpec((1, D, tn), lambda l, n: (l, 0, n)),
                  pl.BlockSpec((1, 1, tn), lambda l, n: (l, 0, n))],
        out_specs=pl.BlockSpec((1, R, tn), lambda l, n: (l, 0, n)),
        out_shape=jax.ShapeDtypeStruct((depth, R, N), f32),
        compiler_params=_params("arbitrary", "arbitrary"),
        name="mod_table",
    )(cvec, w_mod, b_mod.reshape(depth, 1, N))


def _ffn_kernel(x_ref, g_ref, sh_ref, sc_ref, gt_ref, wg_ref, wu_ref, wo_ref, o_ref, hm_ref, *, nf):
    f = pl.program_id(1)

    @pl.when(f == 0)
    def _():
        _modulate_rows(x_ref, g_ref, sh_ref, sc_ref, hm_ref, zero_ref=o_ref)

    h = hm_ref[...]
    gate = _dot(h, wg_ref[...])
    up = _dot(h, wu_ref[...])
    o_ref[...] += _dot((_silu(gate) * up).astype(bf16), wo_ref[...])

    @pl.when(f == nf - 1)
    def _():
        o_ref[...] = x_ref[...] + 0.5 * gt_ref[0] * o_ref[...]


FFN_HIDDEN_TILE = 512


def _ffn(x, g, mods, k0, w_in, w_out, which, T, row_const=None):
    M, D = x.shape
    l, a = which
    F = w_out.shape[2]
    tf = _divisor(F, FFN_HIDDEN_TILE)
    nf = F // tf
    tm, row = _row_tiling(M, T, row_const, 1024)
    return pl.pallas_call(
        functools.partial(_ffn_kernel, nf=nf),
        grid=(M // tm, nf),
        in_specs=[pl.BlockSpec((tm, D), lambda i, f: (i, 0)),
                  pl.BlockSpec((1, D), lambda i, f: (0, 0)),
                  _mod_spec(D, row, k0), _mod_spec(D, row, k0 + 1), _mod_spec(D, row, k0 + 2),
                  pl.BlockSpec((None, None, D, tf), lambda i, f: (l, a, 0, f)),
                  pl.BlockSpec((None, None, D, tf), lambda i, f: (l, a, 0, nf + f)),
                  pl.BlockSpec((None, None, tf, D), lambda i, f: (l, a, f, 0))],
        out_specs=pl.BlockSpec((tm, D), lambda i, f: (i, 0)),
        out_shape=jax.ShapeDtypeStruct((M, D), f32),
        scratch_shapes=[pltpu.VMEM((tm, D), bf16)],
        compiler_params=_params("arbitrary", "arbitrary"),
        name="ffn",
    )(x, g, mods, mods, mods, w_in, w_in, w_out)


def _modulate_once(x_ref, g_ref, sh_ref, sc_ref, hm_ref):
    @pl.when(pl.program_id(1) == 0)
    def _():
        _modulate_rows(x_ref, g_ref, sh_ref, sc_ref, hm_ref)


def _modmm_kernel(x_ref, g_ref, sh_ref, sc_ref, w_ref, o_ref, hm_ref):
    _modulate_once(x_ref, g_ref, sh_ref, sc_ref, hm_ref)
    o_ref[...] = _dot(hm_ref[...], w_ref[...])


def _inproj_specs(D, tm, wn, row, n0=0):
    return [pl.BlockSpec((tm, D), lambda i, n: (i, 0)),
            pl.BlockSpec((1, D), lambda i, n: (0, 0)),
            _mod_spec(D, row, 3), _mod_spec(D, row, 4),
            pl.BlockSpec((D, wn), lambda i, n: (0, n0 + n))]


def _modmm(x, g, mods, w, T, row_const=None, cols=None):
    M, D = x.shape
    c0, c1 = cols if cols is not None else (0, w.shape[1])
    N = c1 - c0
    tm, row = _row_tiling(M, T, row_const, 1024)
    tn = _divisor(math.gcd(N, c0), 1024)
    return pl.pallas_call(
        _modmm_kernel,
        grid=(M // tm, N // tn),
        in_specs=_inproj_specs(D, tm, tn, row, c0 // tn),
        out_specs=pl.BlockSpec((tm, tn), lambda i, n: (i, n)),
        out_shape=jax.ShapeDtypeStruct((M, N), f32),
        scratch_shapes=[pltpu.VMEM((tm, D), bf16)],
        compiler_params=_params("arbitrary", "arbitrary"),
        name="modmm",
    )(x, g, mods, mods, w)


def _evenin_kernel(x_ref, g_ref, sh_ref, sc_ref, w_ref, bg_ref, p_ref, q_ref, hm_ref, *, ts):
    _modulate_once(x_ref, g_ref, sh_ref, sc_ref, hm_ref)
    z = _dot(hm_ref[...], w_ref[...])
    bg_ref[...] = z[:, 0:ts]
    p_ref[...] = z[:, ts:2 * ts] * z[:, 2 * ts:3 * ts]
    q_ref[...] = z[:, 3 * ts:4 * ts] * jax.nn.sigmoid(z[:, 4 * ts:5 * ts])


def _evenin(x, g, mods, w_r, ts, T, row_const=None):
    M, D = x.shape
    S = w_r.shape[1] // 5
    tm, row = _row_tiling(M, T, row_const, 1024)
    out = jax.ShapeDtypeStruct((M, S), f32)
    ospec = pl.BlockSpec((tm, ts), lambda i, n: (i, n))
    return pl.pallas_call(
        functools.partial(_evenin_kernel, ts=ts),
        grid=(M // tm, S // ts),
        in_specs=_inproj_specs(D, tm, 5 * ts, row),
        out_specs=[ospec, ospec, ospec],
        out_shape=[out, out, out],
        scratch_shapes=[pltpu.VMEM((tm, D), bf16)],
        compiler_params=_params("arbitrary", "arbitrary"),
        name="evenin",
    )(x, g, mods, mods, w_r)


def _shifted_taps(win, w_ref, cs, lo, halo, tm):
    acc = None
    for s in range(8):
        taps = [k for k in range(w_ref.shape[0]) if (halo + k - lo + s) % 8 == 0]
        if not taps:
            continue
        wr = win if s == 0 else pltpu.roll(win, s, axis=0)
        for k in taps:
            st = halo + k - lo + s
            term = w_ref[k:k + 1, cs] * wr[st:st + tm]
            acc = term if acc is None else acc + term
    return acc


def _even_out_kernel(bg_ref, pp_ref, p_ref, pn_ref, qp_ref, q_ref, qn_ref, scw_ref, scb_ref, ccw_ref, ccb_ref,
                     lng_ref, lnb_ref, w_ref, x_ref, gt_ref, o_ref, ysc_ref, u_ref, *, tm):
    S = bg_ref.shape[1]
    hp, hq = pp_ref.shape[0], qp_ref.shape[0]
    t = pl.program_id(1)
    first = t == 0
    last = t == pl.num_programs(1) - 1
    lo_sc = (scw_ref.shape[0] - 1) // 2
    lo_cc = (ccw_ref.shape[0] - 1) // 2
    for c in range(S // 128):
        cs = slice(c * 128, (c + 1) * 128)
        pw = jnp.concatenate([jnp.where(first, 0.0, pp_ref[:, cs]), p_ref[:, cs],
                              jnp.where(last, 0.0, pn_ref[:, cs])], axis=0)
        conv = _shifted_taps(pw, scw_ref, cs, lo_sc, hp, tm)
        ysc_ref[:, cs] = (bg_ref[:, cs] * (conv + scb_ref[:, cs])).astype(bf16)
    y = _dot(ysc_ref[...], w_ref[0:S, :])
    for c in range(S // 128):
        cs = slice(c * 128, (c + 1) * 128)
        qw = jnp.concatenate([jnp.where(first, 0.0, qp_ref[:, cs]), q_ref[:, cs],
                              jnp.where(last, 0.0, qn_ref[:, cs])], axis=0)
        u_ref[:, cs] = _shifted_taps(qw, ccw_ref, cs, lo_cc, hq, tm) + ccb_ref[:, cs]
    u = u_ref[...]
    mu = jnp.mean(u, axis=-1, keepdims=True)
    xc = u - mu
    var = jnp.mean(xc * xc, axis=-1, keepdims=True)
    ycc = _silu(xc * lax.rsqrt(var + EPS) * lng_ref[...] + lnb_ref[...]).astype(bf16)
    y = y + _dot(ycc, w_ref[S:2 * S, :])
    o_ref[...] = x_ref[...] + gt_ref[0] * y


def _even_out(bg, p, q, scw, scb, ccw, ccb, lng, lnb, w, x, mods, B, T, row_const=None):
    M, D = x.shape
    S = bg.shape[1]
    hp, hq = 8, 16
    assert (scw.shape[0] - 1) // 2 <= hp and (ccw.shape[0] - 1) // 2 < hq
    tm = _divisor(T, 256)
    nt = T // tm
    rp, rq = tm // hp, tm // hq
    tile = lambda b, t: b * nt + t
    cur = pl.BlockSpec((tm, S), lambda b, t: (tile(b, t), 0))
    prev = lambda h, r: pl.BlockSpec((h, S), lambda b, t: (jnp.maximum(tile(b, t) * r - 1, 0), 0))
    nxt = lambda h, r: pl.BlockSpec((h, S), lambda b, t: (jnp.minimum((tile(b, t) + 1) * r, M // h - 1), 0))
    full = lambda a: pl.BlockSpec(a.shape, lambda b, t: (0,) * a.ndim)
    row = (lambda b: row_const) if row_const is not None else (lambda b: b)
    return pl.pallas_call(
        functools.partial(_even_out_kernel, tm=tm),
        grid=(B, nt),
        in_specs=[cur, prev(hp, rp), cur, nxt(hp, rp), prev(hq, rq), cur, nxt(hq, rq),
                  full(scw), full(scb), full(ccw), full(ccb), full(lng), full(lnb), full(w),
                  pl.BlockSpec((tm, D), lambda b, t: (tile(b, t), 0)),
                  pl.BlockSpec((1, 1, D), lambda b, t: (row(b), 0, 5))],
        out_specs=pl.BlockSpec((tm, D), lambda b, t: (tile(b, t), 0)),
        out_shape=jax.ShapeDtypeStruct((M, D), f32),
        scratch_shapes=[pltpu.VMEM((tm, S), bf16), pltpu.VMEM((tm, S), f32)],
        compiler_params=_params("arbitrary", "arbitrary"),
        name="even_out",
    )(bg, p, p, p, q, q, q, scw, scb, ccw, ccb, lng, lnb, w, x, mods)


def _outproj_odd_kernel(o_ref, hf_ref, hb_ref, gr_ref, wa_ref, wb_ref, x_ref, gt_ref, out_ref):
    r = ((hf_ref[...] + hb_ref[...]) * _gelu_tanh(gr_ref[...])).astype(bf16)
    y = _dot(o_ref[...], wa_ref[...]) + _dot(r, wb_ref[...])
    out_ref[...] = x_ref[...] + gt_ref[0] * y


def _outproj_odd(o, hf, hb, z, gr_block, w, x, mods, T):
    M, D = x.shape
    Ka, Kb = o.shape[1], hf.shape[1]
    tm = _divisor(T, 512)
    nt = T // tm
    return pl.pallas_call(
        _outproj_odd_kernel,
        grid=(M // tm,),
        in_specs=[pl.BlockSpec((tm, Ka), lambda i: (i, 0)),
                  pl.BlockSpec((tm, Kb), lambda i: (i, 0)),
                  pl.BlockSpec((tm, Kb), lambda i: (i, 0)),
                  pl.BlockSpec((tm, Kb), lambda i: (i, gr_block)),
                  pl.BlockSpec((Ka, D), lambda i: (0, 0)),
                  pl.BlockSpec((Kb, D), lambda i: (Ka // Kb, 0)),
                  pl.BlockSpec((tm, D), lambda i: (i, 0)),
                  pl.BlockSpec((1, 1, D), lambda i: (i // nt, 0, 5))],
        out_specs=pl.BlockSpec((tm, D), lambda i: (i, 0)),
        out_shape=jax.ShapeDtypeStruct((M, D), f32),
        compiler_params=_params("arbitrary"),
        name="outproj_odd",
    )(o, hf, hb, z, w, w, x, mods)


def _attn_kernel(q_ref, k_ref, v_ref, kc_ref, vc_ref, qg_ref, kg_ref, bias_ref, o_ref,
                 qn_ref, kn_ref, vn_ref, kcn_ref, vcn_ref, *, rows, kr, win_rows, hd, scale):
    rows_per_trip = _divisor(rows, 16)
    lanes = 2 * hd
    W = GRID_W
    ri = jnp.where(lax.broadcasted_iota(jnp.int32, (lanes, lanes), 0) < hd, 1.0, -1.0)
    ci = jnp.where(lax.broadcasted_iota(jnp.int32, (lanes, lanes), 1) < hd, 1.0, -1.0)
    seg_mean = jnp.where(ri * ci > 0.0, 1.0 / hd, 0.0).astype(bf16)

    def head_norm(x, g):
        sq = x * x
        hi = sq.astype(bf16)
        lo = (sq - hi.astype(f32)).astype(bf16)
        ms = _dot(hi, seg_mean) + _dot(lo, seg_mean)
        return x * lax.rsqrt(ms + EPS) * g

    head0 = lax.broadcasted_iota(jnp.int32, (1, lanes), 1) < hd
    qn = head_norm(q_ref[0], qg_ref[...]) * scale
    q_h0 = jnp.where(head0, qn, 0.0).astype(bf16)
    q_h1 = jnp.where(head0, 0.0, qn).astype(bf16)
    for r in range(rows):
        qn_ref[2 * r * W:(2 * r + 1) * W] = q_h0[r * W:(r + 1) * W]
        qn_ref[(2 * r + 1) * W:(2 * r + 2) * W] = q_h1[r * W:(r + 1) * W]
    kn_ref[...] = head_norm(k_ref[0], kg_ref[...]).astype(bf16)
    kcn_ref[...] = head_norm(kc_ref[0], kg_ref[...]).astype(bf16)
    vn_ref[...] = v_ref[0].astype(bf16)
    vcn_ref[...] = vc_ref[0].astype(bf16)
    nt_dims = (((1,), (1,)), ((), ()))
    span = 2 * W
    trip = rows_per_trip * span

    def body(it, carry):
        q_all = qn_ref[pl.ds(pl.multiple_of(it * trip, trip), trip), :]
        sc_all = lax.dot_general(q_all, kcn_ref[...], nt_dims, preferred_element_type=f32)
        plan, local = [], []
        for sub in range(rows_per_trip):
            r = it * rows_per_trip + sub
            r0 = jnp.clip(r - kr // 2, 0, rows - kr)
            oi = r - r0
            ks = pl.multiple_of(r0 * W, W)
            plan.append((r, ks))
            bias = jnp.concatenate([bias_ref[0, 2 * jj - oi + win_rows - 1] for jj in range(kr // 2)], axis=1)
            kw = kn_ref[pl.ds(ks, kr * W), :]
            local.append(lax.dot_general(q_all[sub * span:(sub + 1) * span], kw, nt_dims,
                                         preferred_element_type=f32) + bias)
        p_local, p_ctx, dens = [], [], []
        for sub in range(rows_per_trip):
            sl = local[sub]
            sc = sc_all[sub * span:(sub + 1) * span]
            m = jnp.maximum(jnp.max(sl, axis=-1, keepdims=True), jnp.max(sc, axis=-1, keepdims=True))
            el = jnp.exp(sl - m)
            ec = jnp.exp(sc - m)
            dens.append(jnp.sum(el, axis=-1, keepdims=True) + jnp.sum(ec, axis=-1, keepdims=True))
            p_local.append(el.astype(bf16))
            p_ctx.append(ec.astype(bf16))
        o_ctx = _dot(jnp.concatenate(p_ctx, axis=0), vcn_ref[...])
        for sub, (r, ks) in enumerate(plan):
            vw = vn_ref[pl.ds(ks, kr * W), :]
            o2 = (_dot(p_local[sub], vw) + o_ctx[sub * span:(sub + 1) * span]) / dens[sub]
            o_ref[0, pl.ds(pl.multiple_of(r * W, W), W), :] = jnp.where(head0, o2[:W], o2[W:]).astype(o_ref.dtype)
        return carry

    lax.fori_loop(0, rows // rows_per_trip, body, 0)


def _bias_table(rpb):
    H, nr, nc = rpb.shape
    Wc = (nc + 1) // 2
    W = GRID_W
    r = rpb.astype(f32)
    period = jnp.concatenate([r[:, :, Wc - 1:], jnp.zeros((H, nr, W + 1 - nc), f32), r[:, :, :Wc - 1]], axis=-1)
    toeplitz = jnp.tile(period, (1, 1, W))[:, :, :W * W].reshape(H, nr, W, W)
    col = jnp.arange(W)
    col_start = jnp.clip(col - Wc // 2, 0, W - Wc)
    col_in = (col[None, :] >= col_start[:, None]) & (col[None, :] < col_start[:, None] + Wc)
    bias_cols = jnp.where(col_in, toeplitz, NEG_INF)
    return jnp.concatenate([bias_cols[:, :-1], bias_cols[:, 1:]], axis=-1)


def _attention(z, zc, qg, kg, rpb, B, L, Cn, n_heads, hd):
    rows = L // GRID_W
    win_rows = (rpb.shape[1] + 1) // 2
    kr = min(win_rows, rows)
    assert kr % 2 == 0
    npair = n_heads // 2
    lanes = 2 * hd
    nd = 2 * win_rows - 2
    bias = _bias_table(rpb).reshape(npair, 2, nd, GRID_W, 2 * GRID_W).transpose(0, 2, 1, 3, 4)
    bias = bias.reshape(npair, nd, 2 * GRID_W, 2 * GRID_W)
    qg2 = jnp.tile(qg, 2).reshape(1, lanes)
    kg2 = jnp.tile(kg, 2).reshape(1, lanes)
    seg = lambda s: pl.BlockSpec((1, L, lanes), lambda p, b: (b, 0, s * npair + p))
    segc = lambda s: pl.BlockSpec((1, Cn, lanes), lambda p, b: (b, 0, s * npair + p))
    return pl.pallas_call(
        functools.partial(_attn_kernel, rows=rows, kr=kr, win_rows=win_rows, hd=hd, scale=hd ** -0.5),
        grid=(npair, B),
        in_specs=[seg(0), seg(1), seg(2), segc(0), segc(1),
                  pl.BlockSpec((1, lanes), lambda p, b: (0, 0)),
                  pl.BlockSpec((1, lanes), lambda p, b: (0, 0)),
                  pl.BlockSpec((1, nd, 2 * GRID_W, 2 * GRID_W), lambda p, b: (p, 0, 0, 0))],
        out_specs=pl.BlockSpec((1, L, lanes), lambda p, b: (b, 0, p)),
        out_shape=jax.ShapeDtypeStruct((B, L, npair * lanes), bf16),
        scratch_shapes=[pltpu.VMEM((2 * L, lanes), bf16), pltpu.VMEM((L, lanes), bf16),
                        pltpu.VMEM((L, lanes), bf16), pltpu.VMEM((Cn, lanes), bf16),
                        pltpu.VMEM((Cn, lanes), bf16)],
        compiler_params=_params("arbitrary", "arbitrary"),
        name="attention",
    )(z, z, z, zc, zc, qg2, kg2, bias)


def _lru_kernel(xfp_ref, xf_ref, xfn_ref, xbp_ref, xb_ref, xbn_ref, h0f_ref, h0b_ref, cw_ref, cb_ref,
                wbd_ref, gb_ref, lam_ref, hf_ref, hb_ref, sf_ref, sb_ref,
                xw_ref, xc_ref, a_ref, u_ref, st_ref, *, tt, tc, nj, chunk):
    B, C1 = xf_ref.shape[1], xf_ref.shape[2]
    kw = cw_ref.shape[0]
    lo = (kw - 1) // 2
    hi = kw - 1 - lo
    j = pl.program_id(0)

    @pl.when(j == 0)
    def _():
        st_ref[0] = h0f_ref[...]
        st_ref[1] = h0b_ref[...]

    for d, (xp_ref, xm_ref, xn_ref, jj) in enumerate(((xfp_ref, xf_ref, xfn_ref, j),
                                                     (xbp_ref, xb_ref, xbn_ref, nj - 1 - j))):
        xw_ref[d, 0:lo] = jnp.where(jj == 0, 0.0, xp_ref[...])
        xw_ref[d, lo:lo + tt] = xm_ref[...]
        xw_ref[d, lo + tt:lo + tt + hi] = jnp.where(jj == nj - 1, 0.0, xn_ref[...])

        def conv_body(ci, carry, d=d):
            t0 = ci * tc
            acc = cw_ref[0][None] * xw_ref[d, pl.ds(t0, tc)]
            for k in range(1, kw):
                acc = acc + cw_ref[k][None] * xw_ref[d, pl.ds(t0 + k, tc)]
            xc_ref[d, pl.ds(t0, tc)] = acc + cb_ref[...][None]
            return carry

        lax.fori_loop(0, tt // tc, conv_body, 0)

        lam = lam_ref[d]
        neg = -lam
        softplus = jnp.maximum(neg, 0.0) + jnp.log(1.0 + jnp.exp(-jnp.abs(neg)))
        rate = -LRU_C * softplus
        for c in range(C1 // chunk):
            cs = slice(c * chunk, (c + 1) * chunk)
            xs = xc_ref[d, :, :, cs].reshape(tt * B, chunk)
            xh = xs.astype(bf16)
            r = _sigmoid(_dot(xh, wbd_ref[d, 0, c]) + gb_ref[d, 0][:, cs])
            i = _sigmoid(_dot(xh, wbd_ref[d, 1, c]) + gb_ref[d, 1][:, cs])
            log_a = rate[:, cs] * r
            a = jnp.exp(log_a)
            mult = jnp.sqrt(jnp.tanh(-log_a) * (a * a + 1.0))
            a_ref[d, :, :, cs] = a.reshape(tt, B, chunk)
            u_ref[d, :, :, cs] = (mult * i * xs).reshape(tt, B, chunk)

    def scan_body(s, carry):
        hf, hb = carry
        tb = tt - 1 - s
        hf = a_ref[0, s] * hf + u_ref[0, s]
        hb = a_ref[1, tb] * hb + u_ref[1, tb]
        hf_ref[s] = hf
        hb_ref[tb] = hb
        return hf, hb

    hf, hb = lax.fori_loop(0, tt, scan_body, (st_ref[0], st_ref[1]), unroll=4)
    st_ref[0] = hf
    st_ref[1] = hb
    sf_ref[...] = hf
    sb_ref[...] = hb


def _lru(x, h0f, h0b, cw, cb, wbd, gb, lam):
    T, B, C1 = x.shape
    kw = cw.shape[0]
    lo = (kw - 1) // 2
    hi = kw - 1 - lo
    assert lo == 1 and hi == 2
    chunk = wbd.shape[-1]
    tt = _divisor(T, 64)
    assert tt % hi == 0
    nj = T // tt
    full = lambda a: pl.BlockSpec(a.shape, lambda j: (0,) * a.ndim)
    state = jax.ShapeDtypeStruct((B, C1), f32)
    seq = jax.ShapeDtypeStruct((T, B, C1), f32)

    def windows(chunk_of):
        return [pl.BlockSpec((lo, B, C1), lambda j: (jnp.maximum(chunk_of(j) * tt - 1, 0), 0, 0)),
                pl.BlockSpec((tt, B, C1), lambda j: (chunk_of(j), 0, 0)),
                pl.BlockSpec((hi, B, C1),
                             lambda j: (jnp.minimum((chunk_of(j) + 1) * (tt // hi), T // hi - 1), 0, 0))]

    fwd = lambda j: j
    bwd = lambda j: nj - 1 - j
    return pl.pallas_call(
        functools.partial(_lru_kernel, tt=tt, tc=8, nj=nj, chunk=chunk),
        grid=(nj,),
        in_specs=windows(fwd) + windows(bwd) + [full(h0f), full(h0b), full(cw), full(cb), full(wbd),
                                                full(gb), full(lam)],
        out_specs=[pl.BlockSpec((tt, B, C1), lambda j: (j, 0, 0)),
                   pl.BlockSpec((tt, B, C1), lambda j: (nj - 1 - j, 0, 0)),
                   pl.BlockSpec((B, C1), lambda j: (0, 0)),
                   pl.BlockSpec((B, C1), lambda j: (0, 0))],
        out_shape=[seq, seq, state, state],
        scratch_shapes=[pltpu.VMEM((2, tt + kw - 1, B, C1), f32), pltpu.VMEM((2, tt, B, C1), f32),
                        pltpu.VMEM((2, tt, B, C1), f32), pltpu.VMEM((2, tt, B, C1), f32),
                        pltpu.VMEM((2, B, C1), f32)],
        compiler_params=_params("arbitrary"),
        name="rglru",
    )(x, x, x, x, x, x, h0f, h0b, cw, cb, wbd, gb, lam)


def _block_diag_chunks(w, chunk):
    *lead, nb, bs, _ = w.shape
    per = chunk // bs
    w6 = w.reshape(*lead, nb // per, per, bs, bs)
    eye = jnp.eye(per, dtype=w.dtype)
    dense = jnp.einsum('...ckij,kl->...ckilj', w6, eye)
    return dense.reshape(*lead, nb // per, chunk, chunk)


def _to_tm(a, B, T):
    return jnp.swapaxes(a.reshape(B, T, a.shape[-1]), 0, 1)


def _to_bm(a):
    T, B, C = a.shape
    return jnp.swapaxes(a, 0, 1).reshape(B * T, C)


def _even_mixer(xs, g, mods, Ts, rows_const, w_in, sc_w, sc_b, cc_w, cc_b, ln_g, ln_b, w_out, B):
    D = w_in.shape[0]
    S = sc_w.shape[1]
    ts = _divisor(S, MXU_DIM)
    w_r = w_in.reshape(D, 5, S // ts, ts).transpose(0, 2, 1, 3).reshape(D, 5 * S).astype(bf16)
    w_o = w_out.astype(bf16)
    row1 = lambda a: a.reshape(1, S)
    outs = []
    for x, T, rc in zip(xs, Ts, rows_const):
        bg, p, q = _evenin(x, g, mods, w_r, ts, T, rc)
        outs.append(_even_out(bg, p, q, sc_w, row1(sc_b), cc_w, row1(cc_b), row1(ln_g), row1(ln_b), w_o, x, mods,
                              B, T, rc))
    return outs


def _odd_mixer_lat(xl, xc, g, mods, B, L, Cn, w_in, q_g, k_g, rpb, conv_w, conv_b, gate_w, gate_b, lam, w_out):
    D = w_in.shape[0]
    hd = q_g.shape[0]
    n_heads = rpb.shape[0]
    Dn = n_heads * hd
    C1 = conv_w.shape[1]
    w = w_in.astype(bf16)
    z = _modmm(xl, g, mods, w, L)
    zc = _modmm(xc, g, mods, w, Cn, row_const=B, cols=(Dn, 3 * Dn + C1))
    o = _attention(z.reshape(B, L, -1), zc.reshape(B, Cn, -1), q_g, k_g, rpb, B, L, Cn, n_heads, hd)
    chunk = min(MXU_DIM, C1)
    wbd = _block_diag_chunks(gate_w, chunk).astype(bf16)
    bcast = lambda a: jnp.broadcast_to(a[..., None, :], a.shape[:-1] + (B, a.shape[-1]))
    cw, cb = bcast(conv_w), bcast(conv_b)
    gb = gate_b[:, :, None, :]
    lam3 = lam[:, None, :]
    zero = jnp.zeros((B, C1), f32)
    xrc = _to_tm(zc[:, 2 * Dn:2 * Dn + C1], B, Cn)
    _, _, sf, sb = _lru(xrc, zero, zero, cw, cb, wbd, gb, lam3)
    xr = _to_tm(z[:, 3 * Dn:3 * Dn + C1], B, L)
    hf, hb, _, _ = _lru(xr, sf, sb, cw, cb, wbd, gb, lam3)
    assert Dn == C1
    return _outproj_odd(o.reshape(B * L, Dn), _to_bm(hf), _to_bm(hb), z, (3 * Dn + C1) // C1,
                        w_out.astype(bf16), xl, mods, L)


def kernel(x, c, ctx, c_ctx, w_mod, b_mod, norm_g, ffn_w_in, ffn_w_out, ev_w_in, sc_w, sc_b, cc_w, cc_b, cc_ln_g, cc_ln_b, ev_w_out, od_w_in, q_norm_g, k_norm_g, na_rpb, lru_conv_w, lru_conv_b, lru_gate_w, lru_gate_b, lru_lam, od_w_out):
    B, L, D = x.shape
    Cn = ctx.shape[1]
    depth = w_mod.shape[0]
    R = -(-(B + 1) // 8) * 8
    cvec = jnp.zeros((R, D), f32).at[:B].set(c).at[B].set(c_ctx)
    mods_all = _mod_table(cvec, w_mod, b_mod)
    xl = x.reshape(B * L, D)
    xc = ctx.reshape(B * Cn, D)
    wi, wo = ffn_w_in.astype(bf16), ffn_w_out.astype(bf16)
    for l in range(depth):
        last = l == depth - 1
        odd = l % 2 == 1
        j = l // 2
        ctx_in = odd or not last
        ctx_out = not last
        mods = mods_all[l].reshape(R, 1, N_MOD * D)
        g = norm_g[l].reshape(3, 1, D)
        xl = _ffn(xl, g[0], mods, 0, wi, wo, (l, 0), L)
        if ctx_in:
            xc = _ffn(xc, g[0], mods, 0, wi, wo, (l, 0), Cn, row_const=B)
        if odd:
            assert ctx_in and not ctx_out
            xl = _odd_mixer_lat(xl, xc, g[1], mods, B, L, Cn, od_w_in[j], q_norm_g[j], k_norm_g[j], na_rpb[j],
                                lru_conv_w[j], lru_conv_b[j], lru_gate_w[j], lru_gate_b[j], lru_lam[j],
                                od_w_out[j])
        else:
            streams = [(xl, L, None)] + ([(xc, Cn, B)] if ctx_out else [])
            outs = _even_mixer([s[0] for s in streams], g[1], mods, [s[1] for s in streams],
                               [s[2] for s in streams], ev_w_in[j], sc_w[j], sc_b[j], cc_w[j], cc_b[j],
                               cc_ln_g[j], cc_ln_b[j], ev_w_out[j], B)
            xl = outs[0]
            if ctx_out:
                xc = outs[1]
        xl = _ffn(xl, g[2], mods, 6, wi, wo, (l, 1), L)
        if ctx_out:
            xc = _ffn(xc, g[2], mods, 6, wi, wo, (l, 1), Cn, row_const=B)
    return xl.reshape(B, L, D)
```

```python
import functools
import math

import jax
import jax.numpy as jnp
from jax import lax
from jax.experimental import pallas as pl
from jax.experimental.pallas import tpu as pltpu

f32 = jnp.float32
bf16 = jnp.bfloat16

GRID_W = 64
LRU_C = 8.0
EPS = 1e-6
NEG_INF = -1e30
N_MOD = 9
MXU_DIM = 256
VMEM_LIMIT_BYTES = 60 * 1024 * 1024


def _divisor(n, pref):
    d = min(n, pref)
    while n % d:
        d -= 1
    return d


def _params(*sem):
    return pltpu.CompilerParams(dimension_semantics=sem, vmem_limit_bytes=VMEM_LIMIT_BYTES)


def _row_tiling(M, T, row_const, pref):
    if row_const is not None:
        return _divisor(M, pref), (lambda i: row_const)
    tm = _divisor(T, pref)
    nt = T // tm
    return tm, (lambda i: i // nt)


def _sigmoid(x):
    return 0.5 * jnp.tanh(0.5 * x) + 0.5


def _silu(x):
    return x * jax.nn.sigmoid(x)


def _gelu_tanh(x):
    return 0.5 * x * (1.0 + jnp.tanh(0.7978845608028654 * (x + 0.044715 * (x * x * x))))


MOD_ROWS = 16
MOD_CHUNKS = 8


def _modulate_rows(x_ref, g_ref, sh_ref, sc_ref, hm_ref, zero_ref=None):
    tm, D = x_ref.shape
    rc = _divisor(tm, MOD_ROWS)
    per_trip = _divisor(tm // rc, MOD_CHUNKS)
    gain = g_ref[...] * (1.0 + sc_ref[0])
    shift = sh_ref[0]

    def body(i, carry):
        for sub in range(per_trip):
            r0 = pl.multiple_of((i * per_trip + sub) * rc, rc)
            x = x_ref[pl.ds(r0, rc), :]
            ms = jnp.mean(x * x, axis=-1, keepdims=True)
            hm_ref[pl.ds(r0, rc), :] = (x * lax.rsqrt(ms + EPS) * gain + shift).astype(bf16)
            if zero_ref is not None:
                zero_ref[pl.ds(r0, rc), :] = jnp.zeros((rc, D), zero_ref.dtype)
        return carry

    lax.fori_loop(0, tm // (rc * per_trip), body, 0)


def _dot(a, b):
    return jnp.dot(a, b, preferred_element_type=f32)


def _mod_spec(D, row, k):
    return pl.BlockSpec((1, 1, D), lambda i, j: (row(i), 0, k))


def _mod_kernel(c_ref, w_ref, b_ref, o_ref):
    s = _silu(c_ref[...]).astype(bf16)
    o_ref[0] = _dot(s, w_ref[0].astype(bf16)) + b_ref[0]


def _mod_table(cvec, w_mod, b_mod):
    depth, D, N = w_mod.shape
    R = cvec.shape[0]
    tn = _divisor(N, 1024)
    return pl.pallas_call(
        _mod_kernel,
        grid=(depth, N // tn),
        in_specs=[pl.BlockSpec((R, D), lambda l, n: (0, 0)),
                  pl.BlockSpec((1, D, tn), lambda l, n: (l, 0, n)),
                  pl.BlockSpec((1, 1, tn), lambda l, n: (l, 0, n))],
        out_specs=pl.BlockSpec((1, R, tn), lambda l, n: (l, 0, n)),
        out_shape=jax.ShapeDtypeStruct((depth, R, N), f32),
        compiler_params=_params("arbitrary", "arbitrary"),
        name="mod_table",
    )(cvec, w_mod, b_mod.reshape(depth, 1, N))


def _ffn_kernel(x_ref, g_ref, sh_ref, sc_ref, gt_ref, wg_ref, wu_ref, wo_ref, o_ref, hm_ref, *, nf):
    f = pl.program_id(1)

    @pl.when(f == 0)
    def _():
        _modulate_rows(x_ref, g_ref, sh_ref, sc_ref, hm_ref, zero_ref=o_ref)

    h = hm_ref[...]
    gate = _dot(h, wg_ref[...])
    up = _dot(h, wu_ref[...])
    o_ref[...] += _dot((_silu(gate) * up).astype(bf16), wo_ref[...])

    @pl.when(f == nf - 1)
    def _():
        o_ref[...] = x_ref[...] + 0.5 * gt_ref[0] * o_ref[...]


FFN_HIDDEN_TILE = 512


def _ffn(x, g, mods, k0, w_in, w_out, which, T, row_const=None):
    M, D = x.shape
    l, a = which
    F = w_out.shape[2]
    tf = _divisor(F, FFN_HIDDEN_TILE)
    nf = F // tf
    tm, row = _row_tiling(M, T, row_const, 1024)
    return pl.pallas_call(
        functools.partial(_ffn_kernel, nf=nf),
        grid=(M // tm, nf),
        in_specs=[pl.BlockSpec((tm, D), lambda i, f: (i, 0)),
                  pl.BlockSpec((1, D), lambda i, f: (0, 0)),
                  _mod_spec(D, row, k0), _mod_spec(D, row, k0 + 1), _mod_spec(D, row, k0 + 2),
                  pl.BlockSpec((None, None, D, tf), lambda i, f: (l, a, 0, f)),
                  pl.BlockSpec((None, None, D, tf), lambda i, f: (l, a, 0, nf + f)),
                  pl.BlockSpec((None, None, tf, D), lambda i, f: (l, a, f, 0))],
        out_specs=pl.BlockSpec((tm, D), lambda i, f: (i, 0)),
        out_shape=jax.ShapeDtypeStruct((M, D), f32),
        scratch_shapes=[pltpu.VMEM((tm, D), bf16)],
        compiler_params=_params("arbitrary", "arbitrary"),
        name="ffn",
    )(x, g, mods, mods, mods, w_in, w_in, w_out)


def _modulate_once(x_ref, g_ref, sh_ref, sc_ref, hm_ref):
    @pl.when(pl.program_id(1) == 0)
    def _():
        _modulate_rows(x_ref, g_ref, sh_ref, sc_ref, hm_ref)


def _modmm_kernel(x_ref, g_ref, sh_ref, sc_ref, w_ref, o_ref, hm_ref):
    _modulate_once(x_ref, g_ref, sh_ref, sc_ref, hm_ref)
    o_ref[...] = _dot(hm_ref[...], w_ref[...])


def _inproj_specs(D, tm, wn, row, n0=0):
    return [pl.BlockSpec((tm, D), lambda i, n: (i, 0)),
            pl.BlockSpec((1, D), lambda i, n: (0, 0)),
            _mod_spec(D, row, 3), _mod_spec(D, row, 4),
            pl.BlockSpec((D, wn), lambda i, n: (0, n0 + n))]


def _modmm_split_kernel(x_ref, g_ref, sh_ref, sc_ref, w_ref, o_ref, d_ref, hm_ref, *, nd):
    _modulate_once(x_ref, g_ref, sh_ref, sc_ref, hm_ref)
    n = pl.program_id(1)

    @pl.when(n != nd)
    def _():
        o_ref[...] = _dot(hm_ref[...], w_ref[...])

    @pl.when(n == nd)
    def _():
        d_ref[...] = _dot(hm_ref[...], w_ref[...])


def _modmm_split(x, g, mods, w, T, seg):
    M, D = x.shape
    N = w.shape[1]
    s0, s1 = seg
    tm, row = _row_tiling(M, T, None, 1024)
    tn = _divisor(math.gcd(N, math.gcd(s0, s1 - s0)), 1024)
    nd = s0 // tn
    assert s1 - s0 == tn and nd > 0
    return pl.pallas_call(
        functools.partial(_modmm_split_kernel, nd=nd),
        grid=(M // tm, N // tn),
        in_specs=_inproj_specs(D, tm, tn, row),
        out_specs=[pl.BlockSpec((tm, tn), lambda i, n: (i, jnp.where(n >= nd, n - 1, n))),
                   pl.BlockSpec((tm, tn), lambda i, n: (i, 0))],
        out_shape=[jax.ShapeDtypeStruct((M, N - tn), f32), jax.ShapeDtypeStruct((M, tn), f32)],
        scratch_shapes=[pltpu.VMEM((tm, D), bf16)],
        compiler_params=_params("arbitrary", "arbitrary"),
        name="modmm_split",
    )(x, g, mods, mods, w)


def _modmm(x, g, mods, w, T, row_const=None, cols=None):
    M, D = x.shape
    c0, c1 = cols if cols is not None else (0, w.shape[1])
    N = c1 - c0
    tm, row = _row_tiling(M, T, row_const, 1024)
    tn = _divisor(math.gcd(N, c0), 1024)
    return pl.pallas_call(
        _modmm_kernel,
        grid=(M // tm, N // tn),
        in_specs=_inproj_specs(D, tm, tn, row, c0 // tn),
        out_specs=pl.BlockSpec((tm, tn), lambda i, n: (i, n)),
        out_shape=jax.ShapeDtypeStruct((M, N), f32),
        scratch_shapes=[pltpu.VMEM((tm, D), bf16)],
        compiler_params=_params("arbitrary", "arbitrary"),
        name="modmm",
    )(x, g, mods, mods, w)


def _evenin_kernel(x_ref, g_ref, sh_ref, sc_ref, w_ref, bg_ref, p_ref, q_ref, hm_ref, *, ts):
    _modulate_once(x_ref, g_ref, sh_ref, sc_ref, hm_ref)
    z = _dot(hm_ref[...], w_ref[...])
    bg_ref[...] = z[:, 0:ts]
    p_ref[...] = z[:, ts:2 * ts] * z[:, 2 * ts:3 * ts]
    q_ref[...] = z[:, 3 * ts:4 * ts] * jax.nn.sigmoid(z[:, 4 * ts:5 * ts])


def _evenin(x, g, mods, w_r, ts, T, row_const=None):
    M, D = x.shape
    S = w_r.shape[1] // 5
    tm, row = _row_tiling(M, T, row_const, 1024)
    out = jax.ShapeDtypeStruct((M, S), f32)
    ospec = pl.BlockSpec((tm, ts), lambda i, n: (i, n))
    return pl.pallas_call(
        functools.partial(_evenin_kernel, ts=ts),
        grid=(M // tm, S // ts),
        in_specs=_inproj_specs(D, tm, 5 * ts, row),
        out_specs=[ospec, ospec, ospec],
        out_shape=[out, out, out],
        scratch_shapes=[pltpu.VMEM((tm, D), bf16)],
        compiler_params=_params("arbitrary", "arbitrary"),
        name="evenin",
    )(x, g, mods, mods, w_r)


def _shifted_taps(win, w_ref, cs, lo, halo, tm):
    acc = None
    for s in range(8):
        taps = [k for k in range(w_ref.shape[0]) if (halo + k - lo + s) % 8 == 0]
        if not taps:
            continue
        wr = win if s == 0 else pltpu.roll(win, s, axis=0)
        for k in taps:
            st = halo + k - lo + s
            term = w_ref[k:k + 1, cs] * wr[st:st + tm]
            acc = term if acc is None else acc + term
    return acc


def _even_out_kernel(bg_ref, pp_ref, p_ref, pn_ref, qp_ref, q_ref, qn_ref, scw_ref, scb_ref, ccw_ref, ccb_ref,
                     lng_ref, lnb_ref, w_ref, x_ref, gt_ref, o_ref, ysc_ref, u_ref, *, tm):
    S = bg_ref.shape[1]
    hp, hq = pp_ref.shape[0], qp_ref.shape[0]
    t = pl.program_id(1)
    first = t == 0
    last = t == pl.num_programs(1) - 1
    lo_sc = (scw_ref.shape[0] - 1) // 2
    lo_cc = (ccw_ref.shape[0] - 1) // 2
    for c in range(S // 128):
        cs = slice(c * 128, (c + 1) * 128)
        pw = jnp.concatenate([jnp.where(first, 0.0, pp_ref[:, cs]), p_ref[:, cs],
                              jnp.where(last, 0.0, pn_ref[:, cs])], axis=0)
        conv = _shifted_taps(pw, scw_ref, cs, lo_sc, hp, tm)
        ysc_ref[:, cs] = (bg_ref[:, cs] * (conv + scb_ref[:, cs])).astype(bf16)
    y = _dot(ysc_ref[...], w_ref[0:S, :])
    for c in range(S // 128):
        cs = slice(c * 128, (c + 1) * 128)
        qw = jnp.concatenate([jnp.where(first, 0.0, qp_ref[:, cs]), q_ref[:, cs],
                              jnp.where(last, 0.0, qn_ref[:, cs])], axis=0)
        u_ref[:, cs] = _shifted_taps(qw, ccw_ref, cs, lo_cc, hq, tm) + ccb_ref[:, cs]
    u = u_ref[...]
    mu = jnp.mean(u, axis=-1, keepdims=True)
    xc = u - mu
    var = jnp.mean(xc * xc, axis=-1, keepdims=True)
    ycc = _silu(xc * lax.rsqrt(var + EPS) * lng_ref[...] + lnb_ref[...]).astype(bf16)
    y = y + _dot(ycc, w_ref[S:2 * S, :])
    o_ref[...] = x_ref[...] + gt_ref[0] * y


def _even_out(bg, p, q, scw, scb, ccw, ccb, lng, lnb, w, x, mods, B, T, row_const=None):
    M, D = x.shape
    S = bg.shape[1]
    hp, hq = 8, 16
    assert (scw.shape[0] - 1) // 2 <= hp and (ccw.shape[0] - 1) // 2 < hq
    tm = _divisor(T, 256)
    nt = T // tm
    rp, rq = tm // hp, tm // hq
    tile = lambda b, t: b * nt + t
    cur = pl.BlockSpec((tm, S), lambda b, t: (tile(b, t), 0))
    prev = lambda h, r: pl.BlockSpec((h, S), lambda b, t: (jnp.maximum(tile(b, t) * r - 1, 0), 0))
    nxt = lambda h, r: pl.BlockSpec((h, S), lambda b, t: (jnp.minimum((tile(b, t) + 1) * r, M // h - 1), 0))
    full = lambda a: pl.BlockSpec(a.shape, lambda b, t: (0,) * a.ndim)
    row = (lambda b: row_const) if row_const is not None else (lambda b: b)
    return pl.pallas_call(
        functools.partial(_even_out_kernel, tm=tm),
        grid=(B, nt),
        in_specs=[cur, prev(hp, rp), cur, nxt(hp, rp), prev(hq, rq), cur, nxt(hq, rq),
                  full(scw), full(scb), full(ccw), full(ccb), full(lng), full(lnb), full(w),
                  pl.BlockSpec((tm, D), lambda b, t: (tile(b, t), 0)),
                  pl.BlockSpec((1, 1, D), lambda b, t: (row(b), 0, 5))],
        out_specs=pl.BlockSpec((tm, D), lambda b, t: (tile(b, t), 0)),
        out_shape=jax.ShapeDtypeStruct((M, D), f32),
        scratch_shapes=[pltpu.VMEM((tm, S), bf16), pltpu.VMEM((tm, S), f32)],
        compiler_params=_params("arbitrary", "arbitrary"),
        name="even_out",
    )(bg, p, p, p, q, q, q, scw, scb, ccw, ccb, lng, lnb, w, x, mods)


def _outproj_odd_kernel(o_ref, hf_ref, hb_ref, gr_ref, wa_ref, wb_ref, x_ref, gt_ref, out_ref):
    r = ((hf_ref[...] + hb_ref[...]) * _gelu_tanh(gr_ref[...])).astype(bf16)
    y = _dot(o_ref[...], wa_ref[...]) + _dot(r, wb_ref[...])
    out_ref[...] = x_ref[...] + gt_ref[0] * y


def _outproj_odd(o, hf, hb, z, gr_block, w, x, mods, T):
    M, D = x.shape
    Ka, Kb = o.shape[1], hf.shape[1]
    tm = _divisor(T, 512)
    nt = T // tm
    return pl.pallas_call(
        _outproj_odd_kernel,
        grid=(M // tm,),
        in_specs=[pl.BlockSpec((tm, Ka), lambda i: (i, 0)),
                  pl.BlockSpec((tm, Kb), lambda i: (i, 0)),
                  pl.BlockSpec((tm, Kb), lambda i: (i, 0)),
                  pl.BlockSpec((tm, Kb), lambda i: (i, gr_block)),
                  pl.BlockSpec((Ka, D), lambda i: (0, 0)),
                  pl.BlockSpec((Kb, D), lambda i: (Ka // Kb, 0)),
                  pl.BlockSpec((tm, D), lambda i: (i, 0)),
                  pl.BlockSpec((1, 1, D), lambda i: (i // nt, 0, 5))],
        out_specs=pl.BlockSpec((tm, D), lambda i: (i, 0)),
        out_shape=jax.ShapeDtypeStruct((M, D), f32),
        compiler_params=_params("arbitrary"),
        name="outproj_odd",
    )(o, hf, hb, z, w, w, x, mods)


def _attn_kernel(q_ref, k_ref, v_ref, kc_ref, vc_ref, qg_ref, kg_ref, bias_ref, o_ref,
                 qn_ref, kn_ref, vn_ref, kcn_ref, vcn_ref, *, rows, kr, win_rows, hd, scale):
    rows_per_trip = _divisor(rows, 16)
    lanes = 2 * hd
    W = GRID_W
    ri = jnp.where(lax.broadcasted_iota(jnp.int32, (lanes, lanes), 0) < hd, 1.0, -1.0)
    ci = jnp.where(lax.broadcasted_iota(jnp.int32, (lanes, lanes), 1) < hd, 1.0, -1.0)
    seg_mean = jnp.where(ri * ci > 0.0, 1.0 / hd, 0.0).astype(bf16)

    def head_norm(x, g):
        sq = x * x
        hi = sq.astype(bf16)
        lo = (sq - hi.astype(f32)).astype(bf16)
        ms = _dot(hi, seg_mean) + _dot(lo, seg_mean)
        return x * lax.rsqrt(ms + EPS) * g

    head0 = lax.broadcasted_iota(jnp.int32, (1, lanes), 1) < hd
    qn = head_norm(q_ref[0], qg_ref[...]) * scale
    q_h0 = jnp.where(head0, qn, 0.0).astype(bf16)
    q_h1 = jnp.where(head0, 0.0, qn).astype(bf16)
    for r in range(rows):
        qn_ref[2 * r * W:(2 * r + 1) * W] = q_h0[r * W:(r + 1) * W]
        qn_ref[(2 * r + 1) * W:(2 * r + 2) * W] = q_h1[r * W:(r + 1) * W]
    kn_ref[...] = head_norm(k_ref[0], kg_ref[...]).astype(bf16)
    kcn_ref[...] = head_norm(kc_ref[0], kg_ref[...]).astype(bf16)
    vn_ref[...] = v_ref[0].astype(bf16)
    vcn_ref[...] = vc_ref[0].astype(bf16)
    nt_dims = (((1,), (1,)), ((), ()))
    span = 2 * W
    trip = rows_per_trip * span

    def body(it, carry):
        q_all = qn_ref[pl.ds(pl.multiple_of(it * trip, trip), trip), :]
        sc_all = lax.dot_general(q_all, kcn_ref[...], nt_dims, preferred_element_type=f32)
        plan, local = [], []
        for sub in range(rows_per_trip):
            r = it * rows_per_trip + sub
            r0 = jnp.clip(r - kr // 2, 0, rows - kr)
            oi = r - r0
            ks = pl.multiple_of(r0 * W, W)
            plan.append((r, ks))
            bias = jnp.concatenate([bias_ref[0, 2 * jj - oi + win_rows - 1] for jj in range(kr // 2)], axis=1)
            kw = kn_ref[pl.ds(ks, kr * W), :]
            local.append(lax.dot_general(q_all[sub * span:(sub + 1) * span], kw, nt_dims,
                                         preferred_element_type=f32) + bias)
        p_local, p_ctx, dens = [], [], []
        for sub in range(rows_per_trip):
            sl = local[sub]
            sc = sc_all[sub * span:(sub + 1) * span]
            m = jnp.maximum(jnp.max(sl, axis=-1, keepdims=True), jnp.max(sc, axis=-1, keepdims=True))
            el = jnp.exp(sl - m)
            ec = jnp.exp(sc - m)
            dens.append(jnp.sum(el, axis=-1, keepdims=True) + jnp.sum(ec, axis=-1, keepdims=True))
            p_local.append(el.astype(bf16))
            p_ctx.append(ec.astype(bf16))
        o_ctx = _dot(jnp.concatenate(p_ctx, axis=0), vcn_ref[...])
        for sub, (r, ks) in enumerate(plan):
            vw = vn_ref[pl.ds(ks, kr * W), :]
            o2 = (_dot(p_local[sub], vw) + o_ctx[sub * span:(sub + 1) * span]) / dens[sub]
            o_ref[0, pl.ds(pl.multiple_of(r * W, W), W), :] = jnp.where(head0, o2[:W], o2[W:]).astype(o_ref.dtype)
        return carry

    lax.fori_loop(0, rows // rows_per_trip, body, 0)


def _bias_table(rpb):
    H, nr, nc = rpb.shape
    Wc = (nc + 1) // 2
    W = GRID_W
    r = rpb.astype(f32)
    period = jnp.concatenate([r[:, :, Wc - 1:], jnp.zeros((H, nr, W + 1 - nc), f32), r[:, :, :Wc - 1]], axis=-1)
    toeplitz = jnp.tile(period, (1, 1, W))[:, :, :W * W].reshape(H, nr, W, W)
    col = jnp.arange(W)
    col_start = jnp.clip(col - Wc // 2, 0, W - Wc)
    col_in = (col[None, :] >= col_start[:, None]) & (col[None, :] < col_start[:, None] + Wc)
    bias_cols = jnp.where(col_in, toeplitz, NEG_INF)
    return jnp.concatenate([bias_cols[:, :-1], bias_cols[:, 1:]], axis=-1)


def _attention(z, zc, qg, kg, rpb, B, L, Cn, n_heads, hd):
    rows = L // GRID_W
    win_rows = (rpb.shape[1] + 1) // 2
    kr = min(win_rows, rows)
    assert kr % 2 == 0
    npair = n_heads // 2
    lanes = 2 * hd
    nd = 2 * win_rows - 2
    bias = _bias_table(rpb).reshape(npair, 2, nd, GRID_W, 2 * GRID_W).transpose(0, 2, 1, 3, 4)
    bias = bias.reshape(npair, nd, 2 * GRID_W, 2 * GRID_W)
    qg2 = jnp.tile(qg, 2).reshape(1, lanes)
    kg2 = jnp.tile(kg, 2).reshape(1, lanes)
    seg = lambda s: pl.BlockSpec((1, L, lanes), lambda p, b: (b, 0, s * npair + p))
    segc = lambda s: pl.BlockSpec((1, Cn, lanes), lambda p, b: (b, 0, s * npair + p))
    return pl.pallas_call(
        functools.partial(_attn_kernel, rows=rows, kr=kr, win_rows=win_rows, hd=hd, scale=hd ** -0.5),
        grid=(npair, B),
        in_specs=[seg(0), seg(1), seg(2), segc(0), segc(1),
                  pl.BlockSpec((1, lanes), lambda p, b: (0, 0)),
                  pl.BlockSpec((1, lanes), lambda p, b: (0, 0)),
                  pl.BlockSpec((1, nd, 2 * GRID_W, 2 * GRID_W), lambda p, b: (p, 0, 0, 0))],
        out_specs=pl.BlockSpec((1, L, lanes), lambda p, b: (b, 0, p)),
        out_shape=jax.ShapeDtypeStruct((B, L, npair * lanes), bf16),
        scratch_shapes=[pltpu.VMEM((2 * L, lanes), bf16), pltpu.VMEM((L, lanes), bf16),
                        pltpu.VMEM((L, lanes), bf16), pltpu.VMEM((Cn, lanes), bf16),
                        pltpu.VMEM((Cn, lanes), bf16)],
        compiler_params=_params("arbitrary", "arbitrary"),
        name="attention",
    )(z, z, z, zc, zc, qg2, kg2, bias)


def _lru_kernel(xfp_ref, xf_ref, xfn_ref, xbp_ref, xb_ref, xbn_ref, h0f_ref, h0b_ref, cw_ref, cb_ref,
                wbd_ref, gb_ref, lam_ref, hf_ref, hb_ref, sf_ref, sb_ref,
                xw_ref, xc_ref, a_ref, u_ref, st_ref, *, tt, tc, nj, chunk):
    B, C1 = xf_ref.shape[1], xf_ref.shape[2]
    kw = cw_ref.shape[0]
    lo = (kw - 1) // 2
    hi = kw - 1 - lo
    j = pl.program_id(0)

    @pl.when(j == 0)
    def _():
        st_ref[0] = h0f_ref[...]
        st_ref[1] = h0b_ref[...]

    for d, (xp_ref, xm_ref, xn_ref, jj) in enumerate(((xfp_ref, xf_ref, xfn_ref, j),
                                                     (xbp_ref, xb_ref, xbn_ref, nj - 1 - j))):
        xw_ref[d, 0:lo] = jnp.where(jj == 0, 0.0, xp_ref[...])
        xw_ref[d, lo:lo + tt] = xm_ref[...]
        xw_ref[d, lo + tt:lo + tt + hi] = jnp.where(jj == nj - 1, 0.0, xn_ref[...])

        def conv_body(ci, carry, d=d):
            t0 = ci * tc
            acc = cw_ref[0][None] * xw_ref[d, pl.ds(t0, tc)]
            for k in range(1, kw):
                acc = acc + cw_ref[k][None] * xw_ref[d, pl.ds(t0 + k, tc)]
            xc_ref[d, pl.ds(t0, tc)] = acc + cb_ref[...][None]
            return carry

        lax.fori_loop(0, tt // tc, conv_body, 0)

        lam = lam_ref[d]
        neg = -lam
        softplus = jnp.maximum(neg, 0.0) + jnp.log(1.0 + jnp.exp(-jnp.abs(neg)))
        rate = -LRU_C * softplus
        for c in range(C1 // chunk):
            cs = slice(c * chunk, (c + 1) * chunk)
            xs = xc_ref[d, :, :, cs].reshape(tt * B, chunk)
            xh = xs.astype(bf16)
            r = _sigmoid(_dot(xh, wbd_ref[d, 0, c]) + gb_ref[d, 0][:, cs])
            i = _sigmoid(_dot(xh, wbd_ref[d, 1, c]) + gb_ref[d, 1][:, cs])
            log_a = rate[:, cs] * r
            a = jnp.exp(log_a)
            mult = jnp.sqrt(jnp.tanh(-log_a) * (a * a + 1.0))
            a_ref[d, :, :, cs] = a.reshape(tt, B, chunk)
            u_ref[d, :, :, cs] = (mult * i * xs).reshape(tt, B, chunk)

    def scan_body(s, carry):
        hf, hb = carry
        tb = tt - 1 - s
        hf = a_ref[0, s] * hf + u_ref[0, s]
        hb = a_ref[1, tb] * hb + u_ref[1, tb]
        hf_ref[s] = hf
        hb_ref[tb] = hb
        return hf, hb

    hf, hb = lax.fori_loop(0, tt, scan_body, (st_ref[0], st_ref[1]), unroll=4)
    st_ref[0] = hf
    st_ref[1] = hb
    sf_ref[...] = hf
    sb_ref[...] = hb


def _lru(x, h0f, h0b, cw, cb, wbd, gb, lam):
    T, B, C1 = x.shape
    kw = cw.shape[0]
    lo = (kw - 1) // 2
    hi = kw - 1 - lo
    assert lo == 1 and hi == 2
    chunk = wbd.shape[-1]
    tt = _divisor(T, 64)
    assert tt % hi == 0
    nj = T // tt
    full = lambda a: pl.BlockSpec(a.shape, lambda j: (0,) * a.ndim)
    state = jax.ShapeDtypeStruct((B, C1), f32)
    seq = jax.ShapeDtypeStruct((T, B, C1), f32)

    def windows(chunk_of):
        return [pl.BlockSpec((lo, B, C1), lambda j: (jnp.maximum(chunk_of(j) * tt - 1, 0), 0, 0)),
                pl.BlockSpec((tt, B, C1), lambda j: (chunk_of(j), 0, 0)),
                pl.BlockSpec((hi, B, C1),
                             lambda j: (jnp.minimum((chunk_of(j) + 1) * (tt // hi), T // hi - 1), 0, 0))]

    fwd = lambda j: j
    bwd = lambda j: nj - 1 - j
    return pl.pallas_call(
        functools.partial(_lru_kernel, tt=tt, tc=8, nj=nj, chunk=chunk),
        grid=(nj,),
        in_specs=windows(fwd) + windows(bwd) + [full(h0f), full(h0b), full(cw), full(cb), full(wbd),
                                                full(gb), full(lam)],
        out_specs=[pl.BlockSpec((tt, B, C1), lambda j: (j, 0, 0)),
                   pl.BlockSpec((tt, B, C1), lambda j: (nj - 1 - j, 0, 0)),
                   pl.BlockSpec((B, C1), lambda j: (0, 0)),
                   pl.BlockSpec((B, C1), lambda j: (0, 0))],
        out_shape=[seq, seq, state, state],
        scratch_shapes=[pltpu.VMEM((2, tt + kw - 1, B, C1), f32), pltpu.VMEM((2, tt, B, C1), f32),
                        pltpu.VMEM((2, tt, B, C1), f32), pltpu.VMEM((2, tt, B, C1), f32),
                        pltpu.VMEM((2, B, C1), f32)],
        compiler_params=_params("arbitrary"),
        name="rglru",
    )(x, x, x, x, x, x, h0f, h0b, cw, cb, wbd, gb, lam)


def _block_diag_chunks(w, chunk):
    *lead, nb, bs, _ = w.shape
    per = chunk // bs
    w6 = w.reshape(*lead, nb // per, per, bs, bs)
    eye = jnp.eye(per, dtype=w.dtype)
    dense = jnp.einsum('...ckij,kl->...ckilj', w6, eye)
    return dense.reshape(*lead, nb // per, chunk, chunk)


def _to_tm(a, B, T):
    return jnp.swapaxes(a.reshape(B, T, a.shape[-1]), 0, 1)


def _to_bm(a):
    T, B, C = a.shape
    return jnp.swapaxes(a, 0, 1).reshape(B * T, C)


def _even_mixer(xs, g, mods, Ts, rows_const, w_in, sc_w, sc_b, cc_w, cc_b, ln_g, ln_b, w_out, B):
    D = w_in.shape[0]
    S = sc_w.shape[1]
    ts = _divisor(S, MXU_DIM)
    w_r = w_in.reshape(D, 5, S // ts, ts).transpose(0, 2, 1, 3).reshape(D, 5 * S).astype(bf16)
    w_o = w_out.astype(bf16)
    row1 = lambda a: a.reshape(1, S)
    outs = []
    for x, T, rc in zip(xs, Ts, rows_const):
        bg, p, q = _evenin(x, g, mods, w_r, ts, T, rc)
        outs.append(_even_out(bg, p, q, sc_w, row1(sc_b), cc_w, row1(cc_b), row1(ln_g), row1(ln_b), w_o, x, mods,
                              B, T, rc))
    return outs


def _odd_mixer_lat(xl, xc, g, mods, B, L, Cn, w_in, q_g, k_g, rpb, conv_w, conv_b, gate_w, gate_b, lam, w_out):
    D = w_in.shape[0]
    hd = q_g.shape[0]
    n_heads = rpb.shape[0]
    Dn = n_heads * hd
    C1 = conv_w.shape[1]
    w = w_in.astype(bf16)
    z, xr_lat = _modmm_split(xl, g, mods, w, L, (3 * Dn, 3 * Dn + C1))
    zc = _modmm(xc, g, mods, w, Cn, row_const=B, cols=(Dn, 3 * Dn + C1))
    o = _attention(z.reshape(B, L, -1), zc.reshape(B, Cn, -1), q_g, k_g, rpb, B, L, Cn, n_heads, hd)
    chunk = min(MXU_DIM, C1)
    wbd = _block_diag_chunks(gate_w, chunk).astype(bf16)
    bcast = lambda a: jnp.broadcast_to(a[..., None, :], a.shape[:-1] + (B, a.shape[-1]))
    cw, cb = bcast(conv_w), bcast(conv_b)
    gb = gate_b[:, :, None, :]
    lam3 = lam[:, None, :]
    zero = jnp.zeros((B, C1), f32)
    xrc = _to_tm(zc[:, 2 * Dn:2 * Dn + C1], B, Cn)
    _, _, sf, sb = _lru(xrc, zero, zero, cw, cb, wbd, gb, lam3)
    xr = _to_tm(xr_lat, B, L)
    hf, hb, _, _ = _lru(xr, sf, sb, cw, cb, wbd, gb, lam3)
    assert Dn == C1
    return _outproj_odd(o.reshape(B * L, Dn), _to_bm(hf), _to_bm(hb), z, 3 * Dn // C1,
                        w_out.astype(bf16), xl, mods, L)


def kernel(x, c, ctx, c_ctx, w_mod, b_mod, norm_g, ffn_w_in, ffn_w_out, ev_w_in, sc_w, sc_b, cc_w, cc_b, cc_ln_g, cc_ln_b, ev_w_out, od_w_in, q_norm_g, k_norm_g, na_rpb, lru_conv_w, lru_conv_b, lru_gate_w, lru_gate_b, lru_lam, od_w_out):
    B, L, D = x.shape
    Cn = ctx.shape[1]
    depth = w_mod.shape[0]
    R = -(-(B + 1) // 8) * 8
    cvec = jnp.zeros((R, D), f32).at[:B].set(c).at[B].set(c_ctx)
    mods_all = _mod_table(cvec, w_mod, b_mod)
    xl = x.reshape(B * L, D)
    xc = ctx.reshape(B * Cn, D)
    wi, wo = ffn_w_in.astype(bf16), ffn_w_out.astype(bf16)
    for l in range(depth):
        last = l == depth - 1
        odd = l % 2 == 1
        j = l // 2
        ctx_in = odd or not last
        ctx_out = not last
        mods = mods_all[l].reshape(R, 1, N_MOD * D)
        g = norm_g[l].reshape(3, 1, D)
        xl = _ffn(xl, g[0], mods, 0, wi, wo, (l, 0), L)
        if ctx_in:
            xc = _ffn(xc, g[0], mods, 0, wi, wo, (l, 0), Cn, row_const=B)
        if odd:
            assert ctx_in and not ctx_out
            xl = _odd_mixer_lat(xl, xc, g[1], mods, B, L, Cn, od_w_in[j], q_norm_g[j], k_norm_g[j], na_rpb[j],
                                lru_conv_w[j], lru_conv_b[j], lru_gate_w[j], lru_gate_b[j], lru_lam[j],
                                od_w_out[j])
        else:
            streams = [(xl, L, None)] + ([(xc, Cn, B)] if ctx_out else [])
            outs = _even_mixer([s[0] for s in streams], g[1], mods, [s[1] for s in streams],
                               [s[2] for s in streams], ev_w_in[j], sc_w[j], sc_b[j], cc_w[j], cc_b[j],
                               cc_ln_g[j], cc_ln_b[j], ev_w_out[j], B)
            xl = outs[0]
            if ctx_out:
                xc = outs[1]
        xl = _ffn(xl, g[2], mods, 6, wi, wo, (l, 1), L)
        if ctx_out:
            xc = _ffn(xc, g[2], mods, 6, wi, wo, (l, 1), Cn, row_const=B)
    return xl.reshape(B, L, D)
```
